```python
import jax, jax.numpy as jnp
from jax import lax
import numpy as np

D_MODEL = 1024
BATCH = 1
SEQ = 16384
DEPTH = 4
DEC_BATCH = 32
DEC_SEQ = 64
PAST_LEN = 4096

CHUNK = 64
N_PAIR = DEPTH // 2
POOL_GROUPS = 4
POOL_GC = 128
POOL_WIDTH = POOL_GROUPS * POOL_GC
POOL_WINDOWS = (2, 4, 8, 16)
POOL_HIST = max(POOL_WINDOWS) - 1
CCV_WIDTH = 512
CCV_K = 31
SGU_HEADS = 4
SGU_HC = 128
SGU_WIDTH = SGU_HEADS * SGU_HC
SGU_CHUNK = 128
SCONV_WIDTH = 512
SCONV_K = 3
D_FF = 2816
FFN_K = 3
EVEN_IN = POOL_WIDTH + 2 * CCV_WIDTH
ODD_IN = 2 * SGU_WIDTH + 3 * SCONV_WIDTH
MIX_OUT = 1024
EPS = 1e-6

kernel_name = 'hybrid_streaming_encoder_step'


def rmsnorm(x, g):
    xf = x.astype(jnp.float32)
    y = xf * lax.rsqrt(jnp.mean(xf * xf, axis=-1, keepdims=True) + EPS)
    return (y * g.astype(jnp.float32)).astype(x.dtype)


def layernorm(x, g, b):
    xf = x.astype(jnp.float32)
    mu = jnp.mean(xf, axis=-1, keepdims=True)
    xc = xf - mu
    var = jnp.mean(xc * xc, axis=-1, keepdims=True)
    y = xc * lax.rsqrt(var + EPS) * g.astype(jnp.float32) + b.astype(jnp.float32)
    return y.astype(x.dtype)


def causal_dwconv(x, hist, w):
    xx = jnp.concatenate([hist.astype(x.dtype), x], axis=1)
    y = lax.conv_general_dilated(xx, w.astype(x.dtype)[:, None, :], (1,), 'VALID',
                                 dimension_numbers=('NWC', 'WIO', 'NWC'),
                                 feature_group_count=x.shape[-1])
    return y, xx[:, xx.shape[1] - (w.shape[0] - 1):]


def multiscale_pool(z, hist, w_pool, scale, pos0):
    B, L, _ = z.shape
    zz = jnp.concatenate([hist.astype(z.dtype), z], axis=1)
    zf = zz.astype(jnp.float32)
    csum = jnp.cumsum(zf, axis=1)
    csum = jnp.concatenate([jnp.zeros_like(csum[:, :1]), csum], axis=1)
    pos = pos0 + jnp.arange(L)
    outs = []
    for g, w in enumerate(POOL_WINDOWS):
        sl = slice(g * POOL_GC, (g + 1) * POOL_GC)
        win_sum = csum[:, POOL_HIST + 1:, sl] - csum[:, POOL_HIST + 1 - w:POOL_HIST + 1 - w + L, sl]
        cnt = jnp.minimum(pos + 1, w).astype(jnp.float32)[None, :, None]
        outs.append(win_sum / cnt - zf[:, POOL_HIST:, sl])
    d = jnp.stack(outs, axis=2)
    y = jnp.einsum('blgc,gcd->blgd', d, w_pool.astype(jnp.float32)).reshape(B, L, POOL_WIDTH)
    y = y * scale.astype(jnp.float32)
    return y.astype(z.dtype), zz[:, L:]


def even_mixer(h, w_in, pool_w, pool_scale, ccv_w, ccv_b, ccv_ln_g, ccv_ln_b, w_out,
               hist_pool, hist_ccv, pos0):
    proj = h @ w_in
    za = proj[..., :POOL_WIDTH]
    a = proj[..., POOL_WIDTH:POOL_WIDTH + CCV_WIDTH]
    gt = proj[..., POOL_WIDTH + CCV_WIDTH:]
    zb = a * jax.nn.sigmoid(gt)
    ya, new_pool = multiscale_pool(za, hist_pool, pool_w, pool_scale, pos0)
    cb, new_ccv = causal_dwconv(zb, hist_ccv, ccv_w)
    yb = jax.nn.silu(layernorm(cb + ccv_b, ccv_ln_g, ccv_ln_b))
    return jnp.concatenate([ya, yb], axis=-1) @ w_out, new_pool, new_ccv


def odd_mixer(h, w_in, sgu_ln_g, sgu_ln_b, sgu_ws, sgu_b, sconv_w, w_out, hist_sconv):
    B, L, _ = h.shape
    proj = h @ w_in
    u = proj[..., :SGU_WIDTH]
    v = layernorm(proj[..., SGU_WIDTH:2 * SGU_WIDTH], sgu_ln_g, sgu_ln_b)
    o = 2 * SGU_WIDTH
    bg = proj[..., o:o + SCONV_WIDTH]
    cg = proj[..., o + SCONV_WIDTH:o + 2 * SCONV_WIDTH]
    xin = proj[..., o + 2 * SCONV_WIDTH:]
    n = -(-L // SGU_CHUNK)
    vp = jnp.pad(v, ((0, 0), (0, n * SGU_CHUNK - L), (0, 0)))
    vp = vp.reshape(B, n, SGU_CHUNK, SGU_HEADS, SGU_HC)
    mask = jnp.tril(jnp.ones((SGU_CHUNK, SGU_CHUNK), dtype=bool))
    ws = jnp.where(mask[None], sgu_ws, jnp.zeros_like(sgu_ws))
    mixed = jnp.einsum('hts,bnshc->bnthc', ws, vp) + sgu_b.T[:, :, None]
    mixed = mixed.reshape(B, n * SGU_CHUNK, SGU_WIDTH)[:, :L]
    yc = u * mixed
    cz, new_sconv = causal_dwconv(cg * xin, hist_sconv, sconv_w)
    yd = bg * cz
    return jnp.concatenate([yc, yd], axis=-1) @ w_out, v, new_sconv


def conv_ffn(h, w_up, conv_w, conv_b, w_down, hist):
    up = h @ w_up
    c, new_hist = causal_dwconv(up, hist, conv_w)
    c = c + conv_b
    return (jax.nn.silu(c[..., :D_FF]) * c[..., D_FF:]) @ w_down, new_hist


def trunk(x, st_pool, st_ccv, st_sconv, st_ffn, pos0,
          norm_mix_g, norm_ffn_g, norm_final_g,
          w_in_even, pool_w, pool_scale, ccv_w, ccv_b, ccv_ln_g, ccv_ln_b, w_out_even,
          w_in_odd, sgu_ln_g, sgu_ln_b, sgu_ws, sgu_b, sconv_w, w_out_odd,
          ffn_w_up, ffn_conv_w, ffn_conv_b, ffn_w_down):
    new_pool, new_ccv, new_sconv, new_v, new_ffn = [], [], [], [], []
    for i in range(DEPTH):
        j = i // 2
        h = rmsnorm(x, norm_mix_g[i])
        if i % 2 == 0:
            y, npool, nccv = even_mixer(h, w_in_even[j], pool_w[j], pool_scale[j], ccv_w[j],
                                        ccv_b[j], ccv_ln_g[j], ccv_ln_b[j], w_out_even[j],
                                        st_pool[j], st_ccv[j], pos0)
            new_pool.append(npool)
            new_ccv.append(nccv)
        else:
            y, v, nsc = odd_mixer(h, w_in_odd[j], sgu_ln_g[j], sgu_ln_b[j], sgu_ws[j], sgu_b[j],
                                  sconv_w[j], w_out_odd[j], st_sconv[j])
            new_v.append(v)
            new_sconv.append(nsc)
        x = x + y
        f, nf = conv_ffn(rmsnorm(x, norm_ffn_g[i]), ffn_w_up[i], ffn_conv_w[i], ffn_conv_b[i],
                         ffn_w_down[i], st_ffn[i])
        new_ffn.append(nf)
        x = x + f
    return (rmsnorm(x, norm_final_g), jnp.stack(new_pool), jnp.stack(new_ccv),
            jnp.stack(new_sconv), jnp.stack(new_v), jnp.stack(new_ffn))


def setup_inputs(seed: int = 0) -> dict:
    key = jax.random.key(seed)
    ks = jax.random.split(key, 32)
    f32 = jnp.float32
    nrm = lambda k, s: jax.random.normal(k, s, f32)
    return {
        'x_prompt': nrm(ks[0], (BATCH, SEQ, D_MODEL)),
        'x_sample': nrm(ks[1], (DEC_BATCH, DEC_SEQ, D_MODEL)),
        'state_pool': nrm(ks[2], (N_PAIR, DEC_BATCH, POOL_HIST, POOL_WIDTH)),
        'state_ccv': 0.5 * nrm(ks[3], (N_PAIR, DEC_BATCH, CCV_K - 1, CCV_WIDTH)),
        'state_sconv': 0.5 * nrm(ks[4], (N_PAIR, DEC_BATCH, SCONV_K - 1, SCONV_WIDTH)),
        'state_ffn_conv': nrm(ks[5], (DEPTH, DEC_BATCH, FFN_K - 1, 2 * D_FF)),
        'norm_mix_g': 1.0 + 0.05 * nrm(ks[6], (DEPTH, D_MODEL)),
        'norm_ffn_g': 1.0 + 0.05 * nrm(ks[7], (DEPTH, D_MODEL)),
        'norm_final_g': 1.0 + 0.05 * nrm(ks[8], (D_MODEL,)),
        'w_in_even': nrm(ks[9], (N_PAIR, D_MODEL, EVEN_IN)) * D_MODEL ** -0.5,
        'pool_w': nrm(ks[10], (N_PAIR, POOL_GROUPS, POOL_GC, POOL_GC)) * POOL_GC ** -0.5,
        'pool_scale': 0.5 * (1.0 + 0.1 * nrm(ks[11], (N_PAIR, POOL_WIDTH))),
        'ccv_w': nrm(ks[12], (N_PAIR, CCV_K, CCV_WIDTH)) * CCV_K ** -0.5,
        'ccv_b': 0.02 * nrm(ks[13], (N_PAIR, CCV_WIDTH)),
        'ccv_ln_g': 1.0 + 0.05 * nrm(ks[14], (N_PAIR, CCV_WIDTH)),
        'ccv_ln_b': 0.02 * nrm(ks[15], (N_PAIR, CCV_WIDTH)),
        'w_out_even': nrm(ks[16], (N_PAIR, MIX_OUT, D_MODEL)) * (0.5 * MIX_OUT ** -0.5),
        'w_in_odd': nrm(ks[17], (N_PAIR, D_MODEL, ODD_IN)) * D_MODEL ** -0.5,
        'sgu_ln_g': 1.0 + 0.05 * nrm(ks[18], (N_PAIR, SGU_WIDTH)),
        'sgu_ln_b': 0.02 * nrm(ks[19], (N_PAIR, SGU_WIDTH)),
        'sgu_ws': nrm(ks[20], (N_PAIR, SGU_HEADS, SGU_CHUNK, SGU_CHUNK)) * SGU_CHUNK ** -0.5,
        'sgu_b': 1.0 + 0.1 * nrm(ks[21], (N_PAIR, SGU_HEADS, SGU_CHUNK)),
        'sconv_w': nrm(ks[22], (N_PAIR, SCONV_K, SCONV_WIDTH)) * SCONV_K ** -0.5,
        'w_out_odd': nrm(ks[23], (N_PAIR, MIX_OUT, D_MODEL)) * (0.5 * MIX_OUT ** -0.5),
        'ffn_w_up': nrm(ks[24], (DEPTH, D_MODEL, 2 * D_FF)) * D_MODEL ** -0.5,
        'ffn_conv_w': nrm(ks[25], (DEPTH, FFN_K, 2 * D_FF)) * FFN_K ** -0.5,
        'ffn_conv_b': 0.02 * nrm(ks[26], (DEPTH, 2 * D_FF)),
        'ffn_w_down': nrm(ks[27], (DEPTH, D_FF, D_MODEL)) * (0.5 * D_FF ** -0.5),
    }


def reference(x_prompt, x_sample, state_pool, state_ccv, state_sconv, state_ffn_conv,
              norm_mix_g, norm_ffn_g, norm_final_g,
              w_in_even, pool_w, pool_scale, ccv_w, ccv_b, ccv_ln_g, ccv_ln_b, w_out_even,
              w_in_odd, sgu_ln_g, sgu_ln_b, sgu_ws, sgu_b, sconv_w, w_out_odd,
              ffn_w_up, ffn_conv_w, ffn_conv_b, ffn_w_down):
    weights = (norm_mix_g, norm_ffn_g, norm_final_g,
               w_in_even, pool_w, pool_scale, ccv_w, ccv_b, ccv_ln_g, ccv_ln_b, w_out_even,
               w_in_odd, sgu_ln_g, sgu_ln_b, sgu_ws, sgu_b, sconv_w, w_out_odd,
               ffn_w_up, ffn_conv_w, ffn_conv_b, ffn_w_down)
    B = x_prompt.shape[0]
    dt = x_prompt.dtype
    z_pool = jnp.zeros((N_PAIR, B, POOL_HIST, POOL_WIDTH), dt)
    z_ccv = jnp.zeros((N_PAIR, B, CCV_K - 1, CCV_WIDTH), dt)
    z_sconv = jnp.zeros((N_PAIR, B, SCONV_K - 1, SCONV_WIDTH), dt)
    z_ffn = jnp.zeros((DEPTH, B, FFN_K - 1, 2 * D_FF), dt)
    y_prompt, pool_p, ccv_p, sconv_p, _, ffn_p = trunk(
        x_prompt, z_pool, z_ccv, z_sconv, z_ffn, 0, *weights)
    y_sample, pool_s, ccv_s, sconv_s, v_s, ffn_s = trunk(
        x_sample, state_pool, state_ccv, state_sconv, state_ffn_conv, PAST_LEN, *weights)
    return (y_prompt, y_sample, pool_p, pool_s, ccv_p, ccv_s, sconv_p, sconv_s, v_s, ffn_p, ffn_s)
```

```python
import functools

import jax
import jax.numpy as jnp
from jax import lax
from jax.experimental import pallas as pl
from jax.experimental.pallas import tpu as pltpu

D_MODEL = 1024
PAST_LEN = 4096
POOL_GC = 128
POOL_WINDOWS = (2, 4, 8, 16)
POOL_WIDTH = 512
CCV_WIDTH = 512
CCV_K = 31
SGU_HEADS = 4
SGU_HC = 128
SGU_WIDTH = 512
SGU_CHUNK = 128
SCONV_WIDTH = 512
SHORT_K = 3
D_FF = 2816
EPS = 1e-6

LANES = 128
SUBLANES = 8
TILE_ROWS = 512
ROW_BLOCK = 32
FF_CHUNK = 256
POOL_HP = 16
CCV_HP = 32
SHORT_HP = 8
VMEM_LIMIT_BYTES = 56 * 1024 * 1024


def _rmsnorm_bf16(x, g):
    y = x * lax.rsqrt(jnp.mean(x * x, axis=-1, keepdims=True) + EPS)
    return (y * g).astype(jnp.bfloat16)


def _layernorm(x, g, b):
    mu = jnp.mean(x, axis=-1, keepdims=True)
    xc = x - mu
    var = jnp.mean(xc * xc, axis=-1, keepdims=True)
    return xc * lax.rsqrt(var + EPS) * g + b


def _silu(x):
    return x * jax.nn.sigmoid(x)


def _seed_state(st_ref, hist_ref, carry):
    if carry:
        @pl.when(pl.program_id(0) == 0)
        def _():
            st_ref[...] = hist_ref[...]
    else:
        st_ref[...] = hist_ref[...]


def _block_rows(i, seg, hp):
    bps = seg // ROW_BLOCK
    s = i // bps
    r = i - s * bps
    xrow = pl.multiple_of(i * ROW_BLOCK, ROW_BLOCK)
    erow = pl.multiple_of(s * (hp + seg) + hp + r * ROW_BLOCK, SUBLANES)
    return xrow, erow


def _delayed(win, ntaps):
    return [win] + [pltpu.roll(win, r, axis=0) for r in range(1, min(ntaps, SUBLANES))]


def _causal_taps(delayed, hp, taps):
    acc = None
    for d, w in enumerate(taps):
        if w is None:
            continue
        q, r = divmod(d, SUBLANES)
        start = hp - q * SUBLANES
        term = w * delayed[r][start:start + ROW_BLOCK]
        acc = term if acc is None else acc + term
    return acc


def _load_hist(ext_ref, st_ref, nseg, seg, hp):
    for s in range(nseg):
        ext_ref[pl.ds(s * (hp + seg), hp), :] = st_ref[s]


def _save_hist(ext_ref, st_ref, nseg, seg, hp):
    for s in range(nseg):
        st_ref[s] = ext_ref[pl.ds(s * (hp + seg) + seg, hp), :]


def _even_kernel(x_ref, g_ref, win_ref, poolw_ref, pscale_ref, ccvw_ref, ccvb_ref, lng_ref,
                 lnb_ref, wout_ref, hpool_ref, hccv_ref,
                 o_ref, stpool_ref, stccv_ref,
                 proj_ref, extp_ref, extc_ref, d_ref, yb_ref,
                 *, nseg, seg, carry, pos0):
    tm = nseg * seg
    _seed_state(stpool_ref, hpool_ref, carry)
    _seed_state(stccv_ref, hccv_ref, carry)

    h = _rmsnorm_bf16(x_ref[...], g_ref[...])
    proj_ref[...] = jnp.dot(h, win_ref[...], preferred_element_type=jnp.float32)

    _load_hist(extp_ref, stpool_ref, nseg, seg, POOL_HP)
    _load_hist(extc_ref, stccv_ref, nseg, seg, CCV_HP)

    def fill(i, c):
        xrow, prow = _block_rows(i, seg, POOL_HP)
        _, crow = _block_rows(i, seg, CCV_HP)
        rows = pl.ds(xrow, ROW_BLOCK)
        extp_ref[pl.ds(prow, ROW_BLOCK), :] = proj_ref[rows, 0:POOL_WIDTH]
        a = proj_ref[rows, POOL_WIDTH:POOL_WIDTH + CCV_WIDTH]
        gt = proj_ref[rows, POOL_WIDTH + CCV_WIDTH:POOL_WIDTH + 2 * CCV_WIDTH]
        extc_ref[pl.ds(crow, ROW_BLOCK), :] = a * jax.nn.sigmoid(gt)
        return c

    lax.fori_loop(0, tm // ROW_BLOCK, fill, 0)
    _save_hist(extp_ref, stpool_ref, nseg, seg, POOL_HP)
    _save_hist(extc_ref, stccv_ref, nseg, seg, CCV_HP)

    tile0 = pl.program_id(0) * tm if carry else 0

    def mix(i, c):
        xrow, prow = _block_rows(i, seg, POOL_HP)
        _, crow = _block_rows(i, seg, CCV_HP)
        rows = pl.ds(xrow, ROW_BLOCK)
        in_seq = (xrow - (xrow // seg) * seg) + lax.broadcasted_iota(jnp.int32, (ROW_BLOCK, 1), 0)
        pos = pos0 + tile0 + in_seq
        for g, w in enumerate(POOL_WINDOWS):
            lanes = pl.ds(g * POOL_GC, POOL_GC)
            z = extp_ref[pl.ds(prow - POOL_HP, POOL_HP + ROW_BLOCK), lanes]
            s, span = z, 1
            while span < min(w, SUBLANES):
                s = s + pltpu.roll(s, span, axis=0)
                span *= 2
            win = s[POOL_HP:]
            for back in range(span, w, span):
                win = win + s[POOL_HP - back:POOL_HP - back + ROW_BLOCK]
            cnt = jnp.minimum(pos + 1, w).astype(jnp.float32)
            dg = win / cnt - z[POOL_HP:]
            d_ref[rows, lanes] = dg.astype(jnp.bfloat16)
        cols = []
        for l in range(CCV_WIDTH // LANES):
            lanes = pl.ds(l * LANES, LANES)
            delayed = _delayed(extc_ref[pl.ds(crow - CCV_HP, CCV_HP + ROW_BLOCK), lanes], CCV_K)
            taps = [ccvw_ref[CCV_K - 1 - d:CCV_K - d, lanes] for d in range(CCV_K)]
            cols.append(_causal_taps(delayed, CCV_HP, taps))
        cb = jnp.concatenate(cols, axis=-1) + ccvb_ref[...]
        yb = _silu(_layernorm(cb, lng_ref[...], lnb_ref[...]))
        yb_ref[rows, :] = yb.astype(jnp.bfloat16)
        return c

    lax.fori_loop(0, tm // ROW_BLOCK, mix, 0)

    y = jnp.dot(yb_ref[...], wout_ref[POOL_WIDTH:, :], preferred_element_type=jnp.float32)
    for g in range(len(POOL_WINDOWS)):
        lanes = slice(g * POOL_GC, (g + 1) * POOL_GC)
        ya = jnp.dot(d_ref[:, lanes], poolw_ref[g], preferred_element_type=jnp.float32)
        ya = (ya * pscale_ref[:, lanes]).astype(jnp.bfloat16)
        y = y + jnp.dot(ya, wout_ref[lanes, :], preferred_element_type=jnp.float32)
    o_ref[...] = x_ref[...] + y


def _odd_kernel(x_ref, g_ref, win_ref, lng_ref, lnb_ref, ws_ref, sb_ref, scw_ref, wout_ref, hsc_ref,
                o_ref, stsc_ref, *rest, nseg, seg, carry, emit_v):
    if emit_v:
        v_ref, proj_ref, exts_ref, vb_ref, yc_ref, yd_ref = rest
    else:
        v_ref = None
        proj_ref, exts_ref, vb_ref, yc_ref, yd_ref = rest
    tm = nseg * seg
    w = SGU_WIDTH
    _seed_state(stsc_ref, hsc_ref, carry)

    h = _rmsnorm_bf16(x_ref[...], g_ref[...])
    proj_ref[...] = jnp.dot(h, win_ref[...], preferred_element_type=jnp.float32)
    _load_hist(exts_ref, stsc_ref, nseg, seg, SHORT_HP)

    def fill(i, c):
        xrow, erow = _block_rows(i, seg, SHORT_HP)
        rows = pl.ds(xrow, ROW_BLOCK)
        v = _layernorm(proj_ref[rows, w:2 * w], lng_ref[...], lnb_ref[...])
        if emit_v:
            v_ref[rows, :] = v
        vb_ref[rows, :] = v.astype(jnp.bfloat16)
        cg = proj_ref[rows, 3 * w:4 * w]
        xin = proj_ref[rows, 4 * w:5 * w]
        exts_ref[pl.ds(erow, ROW_BLOCK), :] = cg * xin
        return c

    lax.fori_loop(0, tm // ROW_BLOCK, fill, 0)
    _save_hist(exts_ref, stsc_ref, nseg, seg, SHORT_HP)

    ch = min(seg, SGU_CHUNK)
    tri = (lax.broadcasted_iota(jnp.int32, (ch, ch), 0) >= lax.broadcasted_iota(jnp.int32, (ch, ch), 1))
    for hd in range(SGU_HEADS):
        lanes = slice(hd * SGU_HC, (hd + 1) * SGU_HC)
        wsm = jnp.where(tri, ws_ref[hd, 0:ch, 0:ch], 0.0).astype(jnp.bfloat16)
        bias = sb_ref[0:ch, hd:hd + 1]
        for n in range(tm // ch):
            rows = slice(n * ch, (n + 1) * ch)
            mixed = jnp.dot(wsm, vb_ref[rows, lanes], preferred_element_type=jnp.float32) + bias
            yc_ref[rows, lanes] = (proj_ref[rows, lanes] * mixed).astype(jnp.bfloat16)

    def sconv(i, c):
        xrow, erow = _block_rows(i, seg, SHORT_HP)
        rows = pl.ds(xrow, ROW_BLOCK)
        delayed = _delayed(exts_ref[pl.ds(erow - SHORT_HP, SHORT_HP + ROW_BLOCK), :], SHORT_K)
        cz = _causal_taps(delayed, SHORT_HP, [scw_ref[SHORT_K - 1 - d:SHORT_K - d, :] for d in range(SHORT_K)])
        yd_ref[rows, :] = (proj_ref[rows, 2 * w:3 * w] * cz).astype(jnp.bfloat16)
        return c

    lax.fori_loop(0, tm // ROW_BLOCK, sconv, 0)

    y = (jnp.dot(yc_ref[...], wout_ref[0:w, :], preferred_element_type=jnp.float32)
         + jnp.dot(yd_ref[...], wout_ref[w:, :], preferred_element_type=jnp.float32))
    o_ref[...] = x_ref[...] + y


def _ffn_kernel(x_ref, g_ref, wup_ref, cw_ref, cb_ref, wdown_ref, gfin_ref, hist_ref,
                o_ref, st_ref, ext_ref, act_ref, *, nseg, seg, carry, final):
    tm = nseg * seg
    _seed_state(st_ref, hist_ref, carry)
    x = x_ref[...]
    h = _rmsnorm_bf16(x, g_ref[...])

    for j in range(D_FF // FF_CHUNK):
        for half in range(2):
            col = half * D_FF + j * FF_CHUNK
            cols = slice(col, col + FF_CHUNK)
            for s in range(nseg):
                ext_ref[half, pl.ds(s * (SHORT_HP + seg), SHORT_HP), :] = st_ref[s, :, cols]
            up = jnp.dot(h, wup_ref[:, cols], preferred_element_type=jnp.float32)
            for s in range(nseg):
                ext_ref[half, pl.ds(s * (SHORT_HP + seg) + SHORT_HP, seg), :] = up[s * seg:(s + 1) * seg]
            for s in range(nseg):
                st_ref[s, :, cols] = ext_ref[half, pl.ds(s * (SHORT_HP + seg) + seg, SHORT_HP), :]

        def act(i, c, j=j):
            xrow, erow = _block_rows(i, seg, SHORT_HP)
            cs = []
            for half in range(2):
                cols = slice(half * D_FF + j * FF_CHUNK, half * D_FF + (j + 1) * FF_CHUNK)
                delayed = _delayed(ext_ref[half, pl.ds(erow - SHORT_HP, SHORT_HP + ROW_BLOCK), :], SHORT_K)
                taps = [cw_ref[SHORT_K - 1 - d:SHORT_K - d, cols] for d in range(SHORT_K)]
                cs.append(_causal_taps(delayed, SHORT_HP, taps) + cb_ref[:, cols])
            a = _silu(cs[0]) * cs[1]
            act_ref[pl.ds(xrow, ROW_BLOCK), j * FF_CHUNK:(j + 1) * FF_CHUNK] = a.astype(jnp.bfloat16)
            return c

        lax.fori_loop(0, tm // ROW_BLOCK, act, 0)

    y = x + jnp.dot(act_ref[...], wdown_ref[...], preferred_element_type=jnp.float32)
    if final:
        y = y * lax.rsqrt(jnp.mean(y * y, axis=-1, keepdims=True) + EPS) * gfin_ref[...]
    o_ref[...] = y


def _const_spec(shape):
    nd = len(shape)
    return pl.BlockSpec(shape, lambda i: (0,) * nd, pipeline_mode=pl.Buffered(1))


def _row_spec(width):
    return pl.BlockSpec((TILE_ROWS, width), lambda i: (i, 0))


def _state_spec(nseg, hp, width, carry):
    if carry:
        return pl.BlockSpec((nseg, hp, width), lambda i: (0, 0, 0))
    return pl.BlockSpec((nseg, hp, width), lambda i: (i, 0, 0))


def _params():
    return pltpu.CompilerParams(dimension_semantics=("arbitrary",), vmem_limit_bytes=VMEM_LIMIT_BYTES)


def _geometry(x, seg):
    rows = x.shape[0]
    carry = seg > TILE_ROWS
    tile_seg = TILE_ROWS if carry else seg
    nseg = TILE_ROWS // tile_seg
    assert rows % TILE_ROWS == 0 and tile_seg % ROW_BLOCK == 0
    return rows, carry, tile_seg, nseg


def _even_call(x, seg, pos0, g, w_in, pool_w, pool_scale, ccv_w, ccv_b, ln_g, ln_b, w_out, h_pool, h_ccv):
    rows, carry, tseg, nseg = _geometry(x, seg)
    nseq = h_pool.shape[0]
    kern = functools.partial(_even_kernel, nseg=nseg, seg=tseg, carry=carry, pos0=pos0)
    consts = (g, w_in, pool_w, pool_scale, ccv_w, ccv_b, ln_g, ln_b, w_out)
    return pl.pallas_call(
        kern,
        grid=(rows // TILE_ROWS,),
        in_specs=[_row_spec(D_MODEL)] + [_const_spec(c.shape) for c in consts]
        + [_state_spec(nseg, POOL_HP, POOL_WIDTH, carry), _state_spec(nseg, CCV_HP, CCV_WIDTH, carry)],
        out_specs=[_row_spec(D_MODEL), _state_spec(nseg, POOL_HP, POOL_WIDTH, carry),
                   _state_spec(nseg, CCV_HP, CCV_WIDTH, carry)],
        out_shape=[jax.ShapeDtypeStruct((rows, D_MODEL), jnp.float32),
                   jax.ShapeDtypeStruct((nseq, POOL_HP, POOL_WIDTH), jnp.float32),
                   jax.ShapeDtypeStruct((nseq, CCV_HP, CCV_WIDTH), jnp.float32)],
        scratch_shapes=[pltpu.VMEM((TILE_ROWS, 3 * 512), jnp.float32),
                        pltpu.VMEM((nseg * (POOL_HP + tseg), POOL_WIDTH), jnp.float32),
                        pltpu.VMEM((nseg * (CCV_HP + tseg), CCV_WIDTH), jnp.float32),
                        pltpu.VMEM((TILE_ROWS, POOL_WIDTH), jnp.bfloat16),
                        pltpu.VMEM((TILE_ROWS, CCV_WIDTH), jnp.bfloat16)],
        compiler_params=_params(),
        name="even_mixer",
    )(x, *consts, h_pool, h_ccv)


def _odd_call(x, seg, emit_v, g, w_in, ln_g, ln_b, ws, sb, sc_w, w_out, h_sc):
    rows, carry, tseg, nseg = _geometry(x, seg)
    nseq = h_sc.shape[0]
    kern = functools.partial(_odd_kernel, nseg=nseg, seg=tseg, carry=carry, emit_v=emit_v)
    consts = (g, w_in, ln_g, ln_b, ws, sb, sc_w, w_out)
    out_specs = [_row_spec(D_MODEL), _state_spec(nseg, SHORT_HP, SCONV_WIDTH, carry)]
    out_shape = [jax.ShapeDtypeStruct((rows, D_MODEL), jnp.float32),
                 jax.ShapeDtypeStruct((nseq, SHORT_HP, SCONV_WIDTH), jnp.float32)]
    if emit_v:
        out_specs.append(_row_spec(SGU_WIDTH))
        out_shape.append(jax.ShapeDtypeStruct((rows, SGU_WIDTH), jnp.float32))
    return pl.pallas_call(
        kern,
        grid=(rows // TILE_ROWS,),
        in_specs=[_row_spec(D_MODEL)] + [_const_spec(c.shape) for c in consts]
        + [_state_spec(nseg, SHORT_HP, SCONV_WIDTH, carry)],
        out_specs=out_specs,
        out_shape=out_shape,
        scratch_shapes=[pltpu.VMEM((TILE_ROWS, 5 * 512), jnp.float32),
                        pltpu.VMEM((nseg * (SHORT_HP + tseg), SCONV_WIDTH), jnp.float32),
                        pltpu.VMEM((TILE_ROWS, SGU_WIDTH), jnp.bfloat16),
                        pltpu.VMEM((TILE_ROWS, SGU_WIDTH), jnp.bfloat16),
                        pltpu.VMEM((TILE_ROWS, SCONV_WIDTH), jnp.bfloat16)],
        compiler_params=_params(),
        name="odd_mixer",
    )(x, *consts, h_sc)


def _ffn_call(x, seg, final, g, w_up, conv_w, conv_b, w_down, g_final, hist):
    rows, carry, tseg, nseg = _geometry(x, seg)
    nseq = hist.shape[0]
    kern = functools.partial(_ffn_kernel, nseg=nseg, seg=tseg, carry=carry, final=final)
    consts = (g, w_up, conv_w, conv_b, w_down, g_final)
    return pl.pallas_call(
        kern,
        grid=(rows // TILE_ROWS,),
        in_specs=[_row_spec(D_MODEL)] + [_const_spec(c.shape) for c in consts]
        + [_state_spec(nseg, SHORT_HP, 2 * D_FF, carry)],
        out_specs=[_row_spec(D_MODEL), _state_spec(nseg, SHORT_HP, 2 * D_FF, carry)],
        out_shape=[jax.ShapeDtypeStruct((rows, D_MODEL), jnp.float32),
                   jax.ShapeDtypeStruct((nseq, SHORT_HP, 2 * D_FF), jnp.float32)],
        scratch_shapes=[pltpu.VMEM((2, nseg * (SHORT_HP + tseg), FF_CHUNK), jnp.float32),
                        pltpu.VMEM((TILE_ROWS, D_FF), jnp.bfloat16)],
        compiler_params=_params(),
        name="conv_ffn",
    )(x, *consts, hist)


def _pad_hist(hist, hp):
    return jnp.pad(hist, ((0, 0), (hp - hist.shape[1], 0), (0, 0)))


def _trunk(x, seg, pos0, emit_v, st_pool, st_ccv, st_sconv, st_ffn, p):
    depth = p["ffn_w_up"].shape[0]
    new_pool, new_ccv, new_sconv, new_v, new_ffn = [], [], [], [], []
    for i in range(depth):
        j = i // 2
        g_mix = p["norm_mix_g"][i][None]
        if i % 2 == 0:
            x, npool, nccv = _even_call(
                x, seg, pos0, g_mix, p["w_in_even"][j], p["pool_w"][j], p["pool_scale"][j][None],
                p["ccv_w"][j], p["ccv_b"][j][None], p["ccv_ln_g"][j][None], p["ccv_ln_b"][j][None],
                p["w_out_even"][j], _pad_hist(st_pool[j], POOL_HP), _pad_hist(st_ccv[j], CCV_HP))
            new_pool.append(npool[:, 1:])
            new_ccv.append(nccv[:, 2:])
        else:
            res = _odd_call(
                x, seg, emit_v, g_mix, p["w_in_odd"][j], p["sgu_ln_g"][j][None], p["sgu_ln_b"][j][None],
                p["sgu_ws"][j], p["sgu_b"][j].T, p["sconv_w"][j], p["w_out_odd"][j],
                _pad_hist(st_sconv[j], SHORT_HP))
            x = res[0]
            new_sconv.append(res[1][:, SHORT_HP - 2:])
            if emit_v:
                new_v.append(res[2])
        x, nffn = _ffn_call(
            x, seg, i == depth - 1, p["norm_ffn_g"][i][None], p["ffn_w_up"][i], p["ffn_conv_w"][i],
            p["ffn_conv_b"][i][None], p["ffn_w_down"][i], p["norm_final_g"][None],
            _pad_hist(st_ffn[i], SHORT_HP))
        new_ffn.append(nffn[:, SHORT_HP - 2:])
    return x, new_pool, new_ccv, new_sconv, new_v, new_ffn


def kernel(x_prompt, x_sample, state_pool, state_ccv, state_sconv, state_ffn_conv, norm_mix_g, norm_ffn_g, norm_final_g, w_in_even, pool_w, pool_scale, ccv_w, ccv_b, ccv_ln_g, ccv_ln_b, w_out_even, w_in_odd, sgu_ln_g, sgu_ln_b, sgu_ws, sgu_b, sconv_w, w_out_odd, ffn_w_up, ffn_conv_w, ffn_conv_b, ffn_w_down):
    bf16 = jnp.bfloat16
    p = dict(norm_mix_g=norm_mix_g, norm_ffn_g=norm_ffn_g, norm_final_g=norm_final_g,
             w_in_even=w_in_even.astype(bf16), pool_w=pool_w.astype(bf16), pool_scale=pool_scale,
             ccv_w=ccv_w, ccv_b=ccv_b, ccv_ln_g=ccv_ln_g, ccv_ln_b=ccv_ln_b,
             w_out_even=w_out_even.astype(bf16), w_in_odd=w_in_odd.astype(bf16),
             sgu_ln_g=sgu_ln_g, sgu_ln_b=sgu_ln_b, sgu_ws=sgu_ws, sgu_b=sgu_b, sconv_w=sconv_w,
             w_out_odd=w_out_odd.astype(bf16), ffn_w_up=ffn_w_up.astype(bf16), ffn_conv_w=ffn_conv_w,
             ffn_conv_b=ffn_conv_b, ffn_w_down=ffn_w_down.astype(bf16))
    n_pair, depth = w_in_even.shape[0], ffn_w_up.shape[0]
    b, seq, _ = x_prompt.shape
    db, dseq, _ = x_sample.shape
    assert b == 1
    f32 = x_prompt.dtype

    zeros = lambda n, k, c: jnp.zeros((n, b, k, c), f32)
    yp, pool_p, ccv_p, sconv_p, _, ffn_p = _trunk(
        x_prompt.reshape(b * seq, D_MODEL), seq, 0, False,
        zeros(n_pair, state_pool.shape[2], POOL_WIDTH), zeros(n_pair, state_ccv.shape[2], CCV_WIDTH),
        zeros(n_pair, state_sconv.shape[2], SCONV_WIDTH), zeros(depth, state_ffn_conv.shape[2], 2 * D_FF), p)
    ys, pool_s, ccv_s, sconv_s, v_s, ffn_s = _trunk(
        x_sample.reshape(db * dseq, D_MODEL), dseq, PAST_LEN, True,
        state_pool, state_ccv, state_sconv, state_ffn_conv, p)
    v_s = [v.reshape(db, dseq, SGU_WIDTH) for v in v_s]
    return (yp.reshape(b, seq, D_MODEL), ys.reshape(db, dseq, D_MODEL),
            jnp.stack(pool_p), jnp.stack(pool_s), jnp.stack(ccv_p), jnp.stack(ccv_s),
            jnp.stack(sconv_p), jnp.stack(sconv_s), jnp.stack(v_s), jnp.stack(ffn_p), jnp.stack(ffn_s))
```

```python
import functools

import jax
import jax.numpy as jnp
from jax import lax
from jax.experimental import pallas as pl
from jax.experimental.pallas import tpu as pltpu

D_MODEL = 1024
PAST_LEN = 4096
POOL_GC = 128
POOL_WINDOWS = (2, 4, 8, 16)
POOL_WIDTH = 512
CCV_WIDTH = 512
CCV_K = 31
SGU_HEADS = 4
SGU_HC = 128
SGU_WIDTH = 512
SGU_CHUNK = 128
SCONV_WIDTH = 512
SHORT_K = 3
D_FF = 2816
EPS = 1e-6

LANES = 128
SUBLANES = 8
TILE_ROWS = 512
ROW_BLOCK = 32
FFN_ROW_BLOCK = 64
FF_CHUNK = 256
POOL_HP = 16
CCV_HP = 32
SHORT_HP = 8
VMEM_LIMIT_BYTES = 56 * 1024 * 1024


def _rmsnorm_bf16(x, g):
    y = x * lax.rsqrt(jnp.mean(x * x, axis=-1, keepdims=True) + EPS)
    return (y * g).astype(jnp.bfloat16)


def _layernorm(x, g, b):
    mu = jnp.mean(x, axis=-1, keepdims=True)
    xc = x - mu
    var = jnp.mean(xc * xc, axis=-1, keepdims=True)
    return xc * lax.rsqrt(var + EPS) * g + b


def _silu(x):
    return x * jax.nn.sigmoid(x)


def _seed_state(st_ref, hist_ref, carry):
    if carry:
        @pl.when(pl.program_id(0) == 0)
        def _():
            st_ref[...] = hist_ref[...]
    else:
        st_ref[...] = hist_ref[...]


def _block_rows(i, seg, hp):
    bps = seg // ROW_BLOCK
    s = i // bps
    r = i - s * bps
    xrow = pl.multiple_of(i * ROW_BLOCK, ROW_BLOCK)
    erow = pl.multiple_of(s * (hp + seg) + hp + r * ROW_BLOCK, SUBLANES)
    return xrow, erow


def _delayed(win, ntaps):
    return [win] + [pltpu.roll(win, r, axis=0) for r in range(1, min(ntaps, SUBLANES))]


def _causal_taps(delayed, hp, taps, nrows=ROW_BLOCK):
    acc = None
    for d, w in enumerate(taps):
        q, r = divmod(d, SUBLANES)
        start = hp - q * SUBLANES
        term = w * delayed[r][start:start + nrows]
        acc = term if acc is None else acc + term
    return acc


def _load_hist(ext_ref, st_ref, nseg, seg, hp):
    for s in range(nseg):
        ext_ref[pl.ds(s * (hp + seg), hp), :] = st_ref[s]


def _save_hist(ext_ref, st_ref, nseg, seg, hp):
    for s in range(nseg):
        st_ref[s] = ext_ref[pl.ds(s * (hp + seg) + seg, hp), :]


def _even_kernel(x_ref, g_ref, win_ref, poolw_ref, pscale_ref, ccvw_ref, ccvb_ref, lng_ref,
                 lnb_ref, wout_ref, hpool_ref, hccv_ref,
                 o_ref, stpool_ref, stccv_ref,
                 proj_ref, extp_ref, extc_ref, d_ref, yb_ref,
                 *, nseg, seg, carry, pos0):
    tm = nseg * seg
    _seed_state(stpool_ref, hpool_ref, carry)
    _seed_state(stccv_ref, hccv_ref, carry)

    h = _rmsnorm_bf16(x_ref[...], g_ref[...])
    proj_ref[...] = jnp.dot(h, win_ref[...], preferred_element_type=jnp.float32)

    _load_hist(extp_ref, stpool_ref, nseg, seg, POOL_HP)
    _load_hist(extc_ref, stccv_ref, nseg, seg, CCV_HP)

    def fill(i, c):
        xrow, prow = _block_rows(i, seg, POOL_HP)
        _, crow = _block_rows(i, seg, CCV_HP)
        rows = pl.ds(xrow, ROW_BLOCK)
        extp_ref[pl.ds(prow, ROW_BLOCK), :] = proj_ref[rows, 0:POOL_WIDTH]
        a = proj_ref[rows, POOL_WIDTH:POOL_WIDTH + CCV_WIDTH]
        gt = proj_ref[rows, POOL_WIDTH + CCV_WIDTH:POOL_WIDTH + 2 * CCV_WIDTH]
        extc_ref[pl.ds(crow, ROW_BLOCK), :] = a * jax.nn.sigmoid(gt)
        return c

    lax.fori_loop(0, tm // ROW_BLOCK, fill, 0)
    _save_hist(extp_ref, stpool_ref, nseg, seg, POOL_HP)
    _save_hist(extc_ref, stccv_ref, nseg, seg, CCV_HP)

    tile0 = pl.program_id(0) * tm if carry else 0

    def mix(i, c):
        xrow, prow = _block_rows(i, seg, POOL_HP)
        _, crow = _block_rows(i, seg, CCV_HP)
        rows = pl.ds(xrow, ROW_BLOCK)
        in_seq = (xrow - (xrow // seg) * seg) + lax.broadcasted_iota(jnp.int32, (ROW_BLOCK, 1), 0)
        pos = pos0 + tile0 + in_seq
        for g, w in enumerate(POOL_WINDOWS):
            lanes = pl.ds(g * POOL_GC, POOL_GC)
            z = extp_ref[pl.ds(prow - POOL_HP, POOL_HP + ROW_BLOCK), lanes]
            s, span = z, 1
            while span < min(w, SUBLANES):
                s = s + pltpu.roll(s, span, axis=0)
                span *= 2
            win = s[POOL_HP:]
            for back in range(span, w, span):
                win = win + s[POOL_HP - back:POOL_HP - back + ROW_BLOCK]
            cnt = jnp.minimum(pos + 1, w).astype(jnp.float32)
            dg = win / cnt - z[POOL_HP:]
            d_ref[rows, lanes] = dg.astype(jnp.bfloat16)
        cols = []
        for l in range(CCV_WIDTH // LANES):
            lanes = pl.ds(l * LANES, LANES)
            delayed = _delayed(extc_ref[pl.ds(crow - CCV_HP, CCV_HP + ROW_BLOCK), lanes], CCV_K)
            taps = [ccvw_ref[CCV_K - 1 - d:CCV_K - d, lanes] for d in range(CCV_K)]
            cols.append(_causal_taps(delayed, CCV_HP, taps))
        cb = jnp.concatenate(cols, axis=-1) + ccvb_ref[...]
        yb = _silu(_layernorm(cb, lng_ref[...], lnb_ref[...]))
        yb_ref[rows, :] = yb.astype(jnp.bfloat16)
        return c

    lax.fori_loop(0, tm // ROW_BLOCK, mix, 0)

    y = jnp.dot(yb_ref[...], wout_ref[POOL_WIDTH:, :], preferred_element_type=jnp.float32)
    for g in range(len(POOL_WINDOWS)):
        lanes = slice(g * POOL_GC, (g + 1) * POOL_GC)
        ya = jnp.dot(d_ref[:, lanes], poolw_ref[g], preferred_element_type=jnp.float32)
        ya = (ya * pscale_ref[:, lanes]).astype(jnp.bfloat16)
        y = y + jnp.dot(ya, wout_ref[lanes, :], preferred_element_type=jnp.float32)
    o_ref[...] = x_ref[...] + y


def _odd_kernel(x_ref, g_ref, win_ref, lng_ref, lnb_ref, ws_ref, sb_ref, scw_ref, wout_ref, hsc_ref,
                o_ref, stsc_ref, *rest, nseg, seg, carry, emit_v):
    if emit_v:
        v_ref, proj_ref, exts_ref, vb_ref, yc_ref, yd_ref = rest
    else:
        v_ref = None
        proj_ref, exts_ref, vb_ref, yc_ref, yd_ref = rest
    tm = nseg * seg
    w = SGU_WIDTH
    _seed_state(stsc_ref, hsc_ref, carry)

    h = _rmsnorm_bf16(x_ref[...], g_ref[...])
    proj_ref[...] = jnp.dot(h, win_ref[...], preferred_element_type=jnp.float32)
    _load_hist(exts_ref, stsc_ref, nseg, seg, SHORT_HP)

    def fill(i, c):
        xrow, erow = _block_rows(i, seg, SHORT_HP)
        rows = pl.ds(xrow, ROW_BLOCK)
        v = _layernorm(proj_ref[rows, w:2 * w], lng_ref[...], lnb_ref[...])
        if emit_v:
            v_ref[rows, :] = v
        vb_ref[rows, :] = v.astype(jnp.bfloat16)
        cg = proj_ref[rows, 3 * w:4 * w]
        xin = proj_ref[rows, 4 * w:5 * w]
        exts_ref[pl.ds(erow, ROW_BLOCK), :] = cg * xin
        return c

    lax.fori_loop(0, tm // ROW_BLOCK, fill, 0)
    _save_hist(exts_ref, stsc_ref, nseg, seg, SHORT_HP)

    ch = min(seg, SGU_CHUNK)
    tri = (lax.broadcasted_iota(jnp.int32, (ch, ch), 0) >= lax.broadcasted_iota(jnp.int32, (ch, ch), 1))
    for hd in range(SGU_HEADS):
        lanes = slice(hd * SGU_HC, (hd + 1) * SGU_HC)
        wsm = jnp.where(tri, ws_ref[hd, 0:ch, 0:ch], 0.0).astype(jnp.bfloat16)
        bias = sb_ref[0:ch, hd:hd + 1]
        for n in range(tm // ch):
            rows = slice(n * ch, (n + 1) * ch)
            mixed = jnp.dot(wsm, vb_ref[rows, lanes], preferred_element_type=jnp.float32) + bias
            yc_ref[rows, lanes] = (proj_ref[rows, lanes] * mixed).astype(jnp.bfloat16)

    def sconv(i, c):
        xrow, erow = _block_rows(i, seg, SHORT_HP)
        rows = pl.ds(xrow, ROW_BLOCK)
        delayed = _delayed(exts_ref[pl.ds(erow - SHORT_HP, SHORT_HP + ROW_BLOCK), :], SHORT_K)
        cz = _causal_taps(delayed, SHORT_HP, [scw_ref[SHORT_K - 1 - d:SHORT_K - d, :] for d in range(SHORT_K)])
        yd_ref[rows, :] = (proj_ref[rows, 2 * w:3 * w] * cz).astype(jnp.bfloat16)
        return c

    lax.fori_loop(0, tm // ROW_BLOCK, sconv, 0)

    y = (jnp.dot(yc_ref[...], wout_ref[0:w, :], preferred_element_type=jnp.float32)
         + jnp.dot(yd_ref[...], wout_ref[w:, :], preferred_element_type=jnp.float32))
    o_ref[...] = x_ref[...] + y


def _ffn_kernel(x_ref, g_ref, wup_ref, cw_ref, cb_ref, wdown_ref, gfin_ref, hist_ref,
                o_ref, st_ref, h_ref, ext_ref, act_ref, *, nseg, seg, carry, final):
    tm = nseg * seg
    nchunk = D_FF // FF_CHUNK
    _seed_state(st_ref, hist_ref, carry)
    h_ref[...] = _rmsnorm_bf16(x_ref[...], g_ref[...])

    def up_chunk(j):
        for half in range(2):
            col = half * D_FF + j * FF_CHUNK
            cols = slice(col, col + FF_CHUNK)
            buf = ext_ref.at[j % 2, half]
            for s in range(nseg):
                buf[pl.ds(s * (SHORT_HP + seg), SHORT_HP), :] = st_ref[s, :, cols]
            up = jnp.dot(h_ref[...], wup_ref[:, cols], preferred_element_type=jnp.float32)
            for s in range(nseg):
                buf[pl.ds(s * (SHORT_HP + seg) + SHORT_HP, seg), :] = up[s * seg:(s + 1) * seg]
            for s in range(nseg):
                st_ref[s, :, cols] = buf[pl.ds(s * (SHORT_HP + seg) + seg, SHORT_HP), :]

    def act_chunk(j):
        for i in range(tm // FFN_ROW_BLOCK):
            xrow = i * FFN_ROW_BLOCK
            s, r = divmod(xrow, seg)
            erow = s * (SHORT_HP + seg) + SHORT_HP + r
            cs = []
            for half in range(2):
                cols = slice(half * D_FF + j * FF_CHUNK, half * D_FF + (j + 1) * FF_CHUNK)
                win = ext_ref[j % 2, half, pl.ds(erow - SHORT_HP, SHORT_HP + FFN_ROW_BLOCK), :]
                taps = [cw_ref[SHORT_K - 1 - d:SHORT_K - d, cols] for d in range(SHORT_K)]
                cs.append(_causal_taps(_delayed(win, SHORT_K), SHORT_HP, taps, FFN_ROW_BLOCK) + cb_ref[:, cols])
            a = _silu(cs[0]) * cs[1]
            act_ref[pl.ds(xrow, FFN_ROW_BLOCK), j * FF_CHUNK:(j + 1) * FF_CHUNK] = a.astype(jnp.bfloat16)

    up_chunk(0)
    for j in range(1, nchunk):
        up_chunk(j)
        act_chunk(j - 1)
    split = (nchunk - 2) * FF_CHUNK
    y = jnp.dot(act_ref[:, :split], wdown_ref[:split, :], preferred_element_type=jnp.float32)
    act_chunk(nchunk - 1)
    y = y + jnp.dot(act_ref[:, split:], wdown_ref[split:, :], preferred_element_type=jnp.float32)
    y = x_ref[...] + y
    if final:
        y = y * lax.rsqrt(jnp.mean(y * y, axis=-1, keepdims=True) + EPS) * gfin_ref[...]
    o_ref[...] = y


def _const_spec(shape):
    nd = len(shape)
    return pl.BlockSpec(shape, lambda i: (0,) * nd, pipeline_mode=pl.Buffered(1))


def _row_spec(width):
    return pl.BlockSpec((TILE_ROWS, width), lambda i: (i, 0))


def _state_spec(nseg, hp, width, carry):
    if carry:
        return pl.BlockSpec((nseg, hp, width), lambda i: (0, 0, 0))
    return pl.BlockSpec((nseg, hp, width), lambda i: (i, 0, 0))


def _params():
    return pltpu.CompilerParams(dimension_semantics=("arbitrary",), vmem_limit_bytes=VMEM_LIMIT_BYTES)


def _geometry(x, seg):
    rows = x.shape[0]
    carry = seg > TILE_ROWS
    tile_seg = TILE_ROWS if carry else seg
    nseg = TILE_ROWS // tile_seg
    assert rows % TILE_ROWS == 0 and tile_seg % ROW_BLOCK == 0
    return rows, carry, tile_seg, nseg


def _even_call(x, seg, pos0, g, w_in, pool_w, pool_scale, ccv_w, ccv_b, ln_g, ln_b, w_out, h_pool, h_ccv):
    rows, carry, tseg, nseg = _geometry(x, seg)
    nseq = h_pool.shape[0]
    kern = functools.partial(_even_kernel, nseg=nseg, seg=tseg, carry=carry, pos0=pos0)
    consts = (g, w_in, pool_w, pool_scale, ccv_w, ccv_b, ln_g, ln_b, w_out)
    return pl.pallas_call(
        kern,
        grid=(rows // TILE_ROWS,),
        in_specs=[_row_spec(D_MODEL)] + [_const_spec(c.shape) for c in consts]
        + [_state_spec(nseg, POOL_HP, POOL_WIDTH, carry), _state_spec(nseg, CCV_HP, CCV_WIDTH, carry)],
        out_specs=[_row_spec(D_MODEL), _state_spec(nseg, POOL_HP, POOL_WIDTH, carry),
                   _state_spec(nseg, CCV_HP, CCV_WIDTH, carry)],
        out_shape=[jax.ShapeDtypeStruct((rows, D_MODEL), jnp.float32),
                   jax.ShapeDtypeStruct((nseq, POOL_HP, POOL_WIDTH), jnp.float32),
                   jax.ShapeDtypeStruct((nseq, CCV_HP, CCV_WIDTH), jnp.float32)],
        scratch_shapes=[pltpu.VMEM((TILE_ROWS, 3 * 512), jnp.float32),
                        pltpu.VMEM((nseg * (POOL_HP + tseg), POOL_WIDTH), jnp.float32),
                        pltpu.VMEM((nseg * (CCV_HP + tseg), CCV_WIDTH), jnp.float32),
                        pltpu.VMEM((TILE_ROWS, POOL_WIDTH), jnp.bfloat16),
                        pltpu.VMEM((TILE_ROWS, CCV_WIDTH), jnp.bfloat16)],
        compiler_params=_params(),
        name="even_mixer",
    )(x, *consts, h_pool, h_ccv)


def _odd_call(x, seg, emit_v, g, w_in, ln_g, ln_b, ws, sb, sc_w, w_out, h_sc):
    rows, carry, tseg, nseg = _geometry(x, seg)
    nseq = h_sc.shape[0]
    kern = functools.partial(_odd_kernel, nseg=nseg, seg=tseg, carry=carry, emit_v=emit_v)
    consts = (g, w_in, ln_g, ln_b, ws, sb, sc_w, w_out)
    out_specs = [_row_spec(D_MODEL), _state_spec(nseg, SHORT_HP, SCONV_WIDTH, carry)]
    out_shape = [jax.ShapeDtypeStruct((rows, D_MODEL), jnp.float32),
                 jax.ShapeDtypeStruct((nseq, SHORT_HP, SCONV_WIDTH), jnp.float32)]
    if emit_v:
        out_specs.append(_row_spec(SGU_WIDTH))
        out_shape.append(jax.ShapeDtypeStruct((rows, SGU_WIDTH), jnp.float32))
    return pl.pallas_call(
        kern,
        grid=(rows // TILE_ROWS,),
        in_specs=[_row_spec(D_MODEL)] + [_const_spec(c.shape) for c in consts]
        + [_state_spec(nseg, SHORT_HP, SCONV_WIDTH, carry)],
        out_specs=out_specs,
        out_shape=out_shape,
        scratch_shapes=[pltpu.VMEM((TILE_ROWS, 5 * 512), jnp.float32),
                        pltpu.VMEM((nseg * (SHORT_HP + tseg), SCONV_WIDTH), jnp.float32),
                        pltpu.VMEM((TILE_ROWS, SGU_WIDTH), jnp.bfloat16),
                        pltpu.VMEM((TILE_ROWS, SGU_WIDTH), jnp.bfloat16),
                        pltpu.VMEM((TILE_ROWS, SCONV_WIDTH), jnp.bfloat16)],
        compiler_params=_params(),
        name="odd_mixer",
    )(x, *consts, h_sc)


def _ffn_call(x, seg, final, g, w_up, conv_w, conv_b, w_down, g_final, hist):
    rows, carry, tseg, nseg = _geometry(x, seg)
    nseq = hist.shape[0]
    kern = functools.partial(_ffn_kernel, nseg=nseg, seg=tseg, carry=carry, final=final)
    consts = (g, w_up, conv_w, conv_b, w_down, g_final)
    return pl.pallas_call(
        kern,
        grid=(rows // TILE_ROWS,),
        in_specs=[_row_spec(D_MODEL)] + [_const_spec(c.shape) for c in consts]
        + [_state_spec(nseg, SHORT_HP, 2 * D_FF, carry)],
        out_specs=[_row_spec(D_MODEL), _state_spec(nseg, SHORT_HP, 2 * D_FF, carry)],
        out_shape=[jax.ShapeDtypeStruct((rows, D_MODEL), jnp.float32),
                   jax.ShapeDtypeStruct((nseq, SHORT_HP, 2 * D_FF), jnp.float32)],
        scratch_shapes=[pltpu.VMEM((TILE_ROWS, D_MODEL), jnp.bfloat16),
                        pltpu.VMEM((2, 2, nseg * (SHORT_HP + tseg), FF_CHUNK), jnp.float32),
                        pltpu.VMEM((TILE_ROWS, D_FF), jnp.bfloat16)],
        compiler_params=_params(),
        name="conv_ffn",
    )(x, *consts, hist)


def _pad_hist(hist, hp):
    return jnp.pad(hist, ((0, 0), (hp - hist.shape[1], 0), (0, 0)))


def _trunk(x, seg, pos0, emit_v, st_pool, st_ccv, st_sconv, st_ffn, p):
    depth = p["ffn_w_up"].shape[0]
    new_pool, new_ccv, new_sconv, new_v, new_ffn = [], [], [], [], []
    for i in range(depth):
        j = i // 2
        g_mix = p["norm_mix_g"][i][None]
        if i % 2 == 0:
            x, npool, nccv = _even_call(
                x, seg, pos0, g_mix, p["w_in_even"][j], p["pool_w"][j], p["pool_scale"][j][None],
                p["ccv_w"][j], p["ccv_b"][j][None], p["ccv_ln_g"][j][None], p["ccv_ln_b"][j][None],
                p["w_out_even"][j], _pad_hist(st_pool[j], POOL_HP), _pad_hist(st_ccv[j], CCV_HP))
            new_pool.append(npool[:, 1:])
            new_ccv.append(nccv[:, 2:])
        else:
            res = _odd_call(
                x, seg, emit_v, g_mix, p["w_in_odd"][j], p["sgu_ln_g"][j][None], p["sgu_ln_b"][j][None],
                p["sgu_ws"][j], p["sgu_b"][j].T, p["sconv_w"][j], p["w_out_odd"][j],
                _pad_hist(st_sconv[j], SHORT_HP))
            x = res[0]
            new_sconv.append(res[1][:, SHORT_HP - 2:])
            if emit_v:
                new_v.append(res[2])
        x, nffn = _ffn_call(
            x, seg, i == depth - 1, p["norm_ffn_g"][i][None], p["ffn_w_up"][i], p["ffn_conv_w"][i],
            p["ffn_conv_b"][i][None], p["ffn_w_down"][i], p["norm_final_g"][None],
            _pad_hist(st_ffn[i], SHORT_HP))
        new_ffn.append(nffn[:, SHORT_HP - 2:])
    return x, new_pool, new_ccv, new_sconv, new_v, new_ffn


def kernel(x_prompt, x_sample, state_pool, state_ccv, state_sconv, state_ffn_conv, norm_mix_g, norm_ffn_g, norm_final_g, w_in_even, pool_w, pool_scale, ccv_w, ccv_b, ccv_ln_g, ccv_ln_b, w_out_even, w_in_odd, sgu_ln_g, sgu_ln_b, sgu_ws, sgu_b, sconv_w, w_out_odd, ffn_w_up, ffn_conv_w, ffn_conv_b, ffn_w_down):
    bf16 = jnp.bfloat16
    p = dict(norm_mix_g=norm_mix_g, norm_ffn_g=norm_ffn_g, norm_final_g=norm_final_g,
             w_in_even=w_in_even.astype(bf16), pool_w=pool_w.astype(bf16), pool_scale=pool_scale,
             ccv_w=ccv_w, ccv_b=ccv_b, ccv_ln_g=ccv_ln_g, ccv_ln_b=ccv_ln_b,
             w_out_even=w_out_even.astype(bf16), w_in_odd=w_in_odd.astype(bf16),
             sgu_ln_g=sgu_ln_g, sgu_ln_b=sgu_ln_b, sgu_ws=sgu_ws, sgu_b=sgu_b, sconv_w=sconv_w,
             w_out_odd=w_out_odd.astype(bf16), ffn_w_up=ffn_w_up.astype(bf16), ffn_conv_w=ffn_conv_w,
             ffn_conv_b=ffn_conv_b, ffn_w_down=ffn_w_down.astype(bf16))
    n_pair, depth = w_in_even.shape[0], ffn_w_up.shape[0]
    b, seq, _ = x_prompt.shape
    db, dseq, _ = x_sample.shape
    assert b == 1
    f32 = x_prompt.dtype

    zeros = lambda n, k, c: jnp.zeros((n, b, k, c), f32)
    yp, pool_p, ccv_p, sconv_p, _, ffn_p = _trunk(
        x_prompt.reshape(b * seq, D_MODEL), seq, 0, False,
        zeros(n_pair, state_pool.shape[2], POOL_WIDTH), zeros(n_pair, state_ccv.shape[2], CCV_WIDTH),
        zeros(n_pair, state_sconv.shape[2], SCONV_WIDTH), zeros(depth, state_ffn_conv.shape[2], 2 * D_FF), p)
    ys, pool_s, ccv_s, sconv_s, v_s, ffn_s = _trunk(
        x_sample.reshape(db * dseq, D_MODEL), dseq, PAST_LEN, True,
        state_pool, state_ccv, state_sconv, state_ffn_conv, p)
    v_s = [v.reshape(db, dseq, SGU_WIDTH) for v in v_s]
    return (yp.reshape(b, seq, D_MODEL), ys.reshape(db, dseq, D_MODEL),
            jnp.stack(pool_p), jnp.stack(pool_s), jnp.stack(ccv_p), jnp.stack(ccv_s),
            jnp.stack(sconv_p), jnp.stack(sconv_s), jnp.stack(v_s), jnp.stack(ffn_p), jnp.stack(ffn_s))
```

```python
import functools

import jax
import jax.numpy as jnp
from jax import lax
from jax.experimental import pallas as pl
from jax.experimental.pallas import tpu as pltpu

D_MODEL = 1024
PAST_LEN = 4096
POOL_GC = 128
POOL_WINDOWS = (2, 4, 8, 16)
POOL_WIDTH = 512
CCV_WIDTH = 512
CCV_K = 31
SGU_HEADS = 4
SGU_HC = 128
SGU_WIDTH = 512
SGU_CHUNK = 128
SCONV_WIDTH = 512
SHORT_K = 3
D_FF = 2816
EPS = 1e-6

LANES = 128
SUBLANES = 8
TILE_ROWS = 512
SUB_ROWS = 256
ROW_BLOCK = 64
NORM_ROWS = 32
FF_CHUNK = 256
POOL_HP = 16
CCV_HP = 32
SHORT_HP = 8
VMEM_LIMIT_BYTES = 56 * 1024 * 1024


def _rmsnorm_bf16(x, g):
    y = x * lax.rsqrt(jnp.mean(x * x, axis=-1, keepdims=True) + EPS)
    return (y * g).astype(jnp.bfloat16)


def _layernorm(x, g, b):
    mu = jnp.mean(x, axis=-1, keepdims=True)
    xc = x - mu
    var = jnp.mean(xc * xc, axis=-1, keepdims=True)
    return xc * lax.rsqrt(var + EPS) * g + b


def _silu(x):
    return x * jax.nn.sigmoid(x)


def _seed_state(st_ref, hist_ref, carry):
    if carry:
        @pl.when(pl.program_id(0) == 0)
        def _():
            st_ref[...] = hist_ref[...]
    else:
        st_ref[...] = hist_ref[...]


def _ext_row(row, seg, hp):
    s, r = divmod(row, seg)
    return s * (hp + seg) + hp + r


def _store_rows(ext_ref, val, row0, seg, hp):
    n = val.shape[0]
    run = min(seg, n)
    for off in range(0, n, run):
        ext_ref[pl.ds(_ext_row(row0 + off, seg, hp), run), :] = val[off:off + run]


def _causal_conv(win, hp, nrows, taps):
    acc = None
    for r in range(min(len(taps), SUBLANES)):
        rolled = win if r == 0 else pltpu.roll(win, r, axis=0)
        for d in range(r, len(taps), SUBLANES):
            start = hp - (d - r)
            term = taps[d] * rolled[start:start + nrows]
            acc = term if acc is None else acc + term
    return acc


def _load_hist(ext_ref, st_ref, nseg, seg, hp):
    for s in range(nseg):
        ext_ref[pl.ds(s * (hp + seg), hp), :] = st_ref[s]


def _save_hist(ext_ref, st_ref, nseg, seg, hp):
    for s in range(nseg):
        st_ref[s] = ext_ref[pl.ds(s * (hp + seg) + seg, hp), :]


def _pipelined(nsub, stage_in, after_in, stage_mix, stage_out):
    stage_in(0)
    for k in range(nsub):
        if k + 1 < nsub:
            stage_in(k + 1)
        if k == max(nsub - 2, 0):
            after_in()
        stage_mix(k)
        stage_out(k)


def _even_kernel(x_ref, g_ref, win_ref, poolw_ref, pscale_ref, ccvw_ref, ccvb_ref, lng_ref,
                 lnb_ref, wout_ref, hpool_ref, hccv_ref,
                 o_ref, stpool_ref, stccv_ref,
                 extp_ref, extc_ref, cb_ref, cat_ref,
                 *, nseg, seg, carry, pos0):
    tm = nseg * seg
    _seed_state(stpool_ref, hpool_ref, carry)
    _seed_state(stccv_ref, hccv_ref, carry)
    _load_hist(extp_ref, stpool_ref, nseg, seg, POOL_HP)
    _load_hist(extc_ref, stccv_ref, nseg, seg, CCV_HP)
    tile0 = pl.program_id(0) * tm if carry else 0

    def stage_in(k):
        r0 = k * SUB_ROWS
        h = _rmsnorm_bf16(x_ref[r0:r0 + SUB_ROWS, :], g_ref[...])
        proj = jnp.dot(h, win_ref[...], preferred_element_type=jnp.float32)
        _store_rows(extp_ref, proj[:, :POOL_WIDTH], r0, seg, POOL_HP)
        a = proj[:, POOL_WIDTH:POOL_WIDTH + CCV_WIDTH]
        gt = proj[:, POOL_WIDTH + CCV_WIDTH:]
        _store_rows(extc_ref, a * jax.nn.sigmoid(gt), r0, seg, CCV_HP)

    def after_in():
        _save_hist(extp_ref, stpool_ref, nseg, seg, POOL_HP)
        _save_hist(extc_ref, stccv_ref, nseg, seg, CCV_HP)

    def stage_mix(k):
        for b in range(SUB_ROWS // ROW_BLOCK):
            row = k * SUB_ROWS + b * ROW_BLOCK
            rows = slice(row, row + ROW_BLOCK)
            in_seq = row % seg
            prow = _ext_row(row, seg, POOL_HP)
            crow = _ext_row(row, seg, CCV_HP)
            for g, w in enumerate(POOL_WINDOWS):
                lanes = slice(g * POOL_GC, (g + 1) * POOL_GC)
                z = extp_ref[prow - POOL_HP:prow + ROW_BLOCK, lanes]
                s, span = z, 1
                while span < min(w, SUBLANES):
                    s = s + pltpu.roll(s, span, axis=0)
                    span *= 2
                win = s[POOL_HP:]
                for back in range(span, w, span):
                    win = win + s[POOL_HP - back:POOL_HP - back + ROW_BLOCK]
                if pos0 + in_seq >= w - 1:
                    mean = win * (1.0 / w)
                else:
                    pos = pos0 + tile0 + in_seq + lax.broadcasted_iota(jnp.int32, (ROW_BLOCK, POOL_GC), 0)
                    mean = win / jnp.minimum(pos + 1, w).astype(jnp.float32)
                cat_ref[rows, lanes] = (mean - z[POOL_HP:]).astype(jnp.bfloat16)
            for l in range(CCV_WIDTH // LANES):
                lanes = slice(l * LANES, (l + 1) * LANES)
                win = extc_ref[crow - CCV_HP:crow + ROW_BLOCK, lanes]
                taps = [ccvw_ref[CCV_K - 1 - d:CCV_K - d, lanes] for d in range(CCV_K)]
                cb_ref[rows, lanes] = _causal_conv(win, CCV_HP, ROW_BLOCK, taps) + ccvb_ref[:, lanes]
            for n in range(row, row + ROW_BLOCK, NORM_ROWS):
                yb = _silu(_layernorm(cb_ref[n:n + NORM_ROWS, :], lng_ref[...], lnb_ref[...]))
                cat_ref[n:n + NORM_ROWS, POOL_WIDTH:] = yb.astype(jnp.bfloat16)

    def stage_out(k):
        rows = slice(k * SUB_ROWS, (k + 1) * SUB_ROWS)
        for g in range(len(POOL_WINDOWS)):
            lanes = slice(g * POOL_GC, (g + 1) * POOL_GC)
            ya = jnp.dot(cat_ref[rows, lanes], poolw_ref[g], preferred_element_type=jnp.float32)
            cat_ref[rows, lanes] = (ya * pscale_ref[:, lanes]).astype(jnp.bfloat16)
        y = jnp.dot(cat_ref[rows, :], wout_ref[...], preferred_element_type=jnp.float32)
        o_ref[rows, :] = x_ref[rows, :] + y

    _pipelined(tm // SUB_ROWS, stage_in, after_in, stage_mix, stage_out)


def _odd_kernel(x_ref, g_ref, win_ref, lng_ref, lnb_ref, ws_ref, sb_ref, scw_ref, wout_ref, hsc_ref,
                o_ref, stsc_ref, *rest, nseg, seg, carry, emit_v):
    if emit_v:
        v_ref, ug_ref, exts_ref, vb_ref, cat_ref, wsm_ref = rest
    else:
        v_ref = None
        ug_ref, exts_ref, vb_ref, cat_ref, wsm_ref = rest
    tm = nseg * seg
    w = SGU_WIDTH
    _seed_state(stsc_ref, hsc_ref, carry)
    _load_hist(exts_ref, stsc_ref, nseg, seg, SHORT_HP)

    ch = min(seg, SGU_CHUNK)
    tri = (lax.broadcasted_iota(jnp.int32, (ch, ch), 0) >= lax.broadcasted_iota(jnp.int32, (ch, ch), 1))
    for hd in range(SGU_HEADS):
        wsm_ref[hd] = jnp.where(tri, ws_ref[hd, 0:ch, 0:ch], 0.0).astype(jnp.bfloat16)

    def stage_in(k):
        r0 = k * SUB_ROWS
        h = _rmsnorm_bf16(x_ref[r0:r0 + SUB_ROWS, :], g_ref[...])
        proj = jnp.dot(h, win_ref[...], preferred_element_type=jnp.float32)
        ug_ref[r0:r0 + SUB_ROWS, 0:w] = proj[:, 0:w]
        ug_ref[r0:r0 + SUB_ROWS, w:] = proj[:, 2 * w:3 * w]
        for n in range(0, SUB_ROWS, NORM_ROWS):
            v = _layernorm(proj[n:n + NORM_ROWS, w:2 * w], lng_ref[...], lnb_ref[...])
            if emit_v:
                v_ref[r0 + n:r0 + n + NORM_ROWS, :] = v
            vb_ref[r0 + n:r0 + n + NORM_ROWS, :] = v.astype(jnp.bfloat16)
        _store_rows(exts_ref, proj[:, 3 * w:4 * w] * proj[:, 4 * w:], r0, seg, SHORT_HP)

    def after_in():
        _save_hist(exts_ref, stsc_ref, nseg, seg, SHORT_HP)

    def stage_mix(k):
        for hd in range(SGU_HEADS):
            lanes = slice(hd * SGU_HC, (hd + 1) * SGU_HC)
            bias = jnp.broadcast_to(sb_ref[0:ch, hd:hd + 1], (ch, SGU_HC))
            for row in range(k * SUB_ROWS, (k + 1) * SUB_ROWS, ch):
                rows = slice(row, row + ch)
                mixed = jnp.dot(wsm_ref[hd], vb_ref[rows, lanes], preferred_element_type=jnp.float32) + bias
                cat_ref[rows, lanes] = (ug_ref[rows, lanes] * mixed).astype(jnp.bfloat16)
        for row in range(k * SUB_ROWS, (k + 1) * SUB_ROWS, ROW_BLOCK):
            rows = slice(row, row + ROW_BLOCK)
            erow = _ext_row(row, seg, SHORT_HP)
            for l in range(SCONV_WIDTH // LANES):
                lanes = slice(l * LANES, (l + 1) * LANES)
                win = exts_ref[erow - SHORT_HP:erow + ROW_BLOCK, lanes]
                taps = [scw_ref[SHORT_K - 1 - d:SHORT_K - d, lanes] for d in range(SHORT_K)]
                cz = _causal_conv(win, SHORT_HP, ROW_BLOCK, taps)
                cat_ref[rows, w + l * LANES:w + (l + 1) * LANES] = (
                    ug_ref[rows, w + l * LANES:w + (l + 1) * LANES] * cz).astype(jnp.bfloat16)

    def stage_out(k):
        rows = slice(k * SUB_ROWS, (k + 1) * SUB_ROWS)
        y = jnp.dot(cat_ref[rows, :], wout_ref[...], preferred_element_type=jnp.float32)
        o_ref[rows, :] = x_ref[rows, :] + y

    _pipelined(tm // SUB_ROWS, stage_in, after_in, stage_mix, stage_out)


def _ffn_kernel(x_ref, g_ref, wup_ref, cw_ref, cb_ref, wdown_ref, gfin_ref, hist_ref,
                o_ref, st_ref, h_ref, ext_ref, act_ref, *, nseg, seg, carry, final):
    tm = nseg * seg
    nchunk = D_FF // FF_CHUNK
    _seed_state(st_ref, hist_ref, carry)
    h_ref[...] = _rmsnorm_bf16(x_ref[...], g_ref[...])

    def up_chunk(j):
        for half in range(2):
            col = half * D_FF + j * FF_CHUNK
            cols = slice(col, col + FF_CHUNK)
            buf = ext_ref.at[j % 2, half]
            for s in range(nseg):
                buf[pl.ds(s * (SHORT_HP + seg), SHORT_HP), :] = st_ref[s, :, cols]
            up = jnp.dot(h_ref[...], wup_ref[:, cols], preferred_element_type=jnp.float32)
            _store_rows(buf, up, 0, seg, SHORT_HP)
            for s in range(nseg):
                st_ref[s, :, cols] = buf[pl.ds(s * (SHORT_HP + seg) + seg, SHORT_HP), :]

    def act_chunk(j):
        for row in range(0, tm, ROW_BLOCK):
            erow = _ext_row(row, seg, SHORT_HP)
            cs = []
            for half in range(2):
                cols = slice(half * D_FF + j * FF_CHUNK, half * D_FF + (j + 1) * FF_CHUNK)
                win = ext_ref[j % 2, half, erow - SHORT_HP:erow + ROW_BLOCK, :]
                taps = [cw_ref[SHORT_K - 1 - d:SHORT_K - d, cols] for d in range(SHORT_K)]
                cs.append(_causal_conv(win, SHORT_HP, ROW_BLOCK, taps) + cb_ref[:, cols])
            a = _silu(cs[0]) * cs[1]
            act_ref[row:row + ROW_BLOCK, j * FF_CHUNK:(j + 1) * FF_CHUNK] = a.astype(jnp.bfloat16)

    up_chunk(0)
    for j in range(1, nchunk):
        up_chunk(j)
        act_chunk(j - 1)
    split = (nchunk - 2) * FF_CHUNK
    y = jnp.dot(act_ref[:, :split], wdown_ref[:split, :], preferred_element_type=jnp.float32)
    act_chunk(nchunk - 1)
    y = y + jnp.dot(act_ref[:, split:], wdown_ref[split:, :], preferred_element_type=jnp.float32)
    y = x_ref[...] + y
    if final:
        y = y * lax.rsqrt(jnp.mean(y * y, axis=-1, keepdims=True) + EPS) * gfin_ref[...]
    o_ref[...] = y


def _layer_spec(arr, layer):
    zeros = (0,) * (arr.ndim - 1)
    return pl.BlockSpec((None,) + arr.shape[1:], lambda i: (layer,) + zeros, pipeline_mode=pl.Buffered(1))


def _row_spec(width):
    return pl.BlockSpec((TILE_ROWS, width), lambda i: (i, 0))


def _state_spec(nseg, hp, width, carry):
    if carry:
        return pl.BlockSpec((nseg, hp, width), lambda i: (0, 0, 0))
    return pl.BlockSpec((nseg, hp, width), lambda i: (i, 0, 0))


def _params():
    return pltpu.CompilerParams(dimension_semantics=("arbitrary",), vmem_limit_bytes=VMEM_LIMIT_BYTES)


def _geometry(x, seg):
    rows = x.shape[0]
    carry = seg > TILE_ROWS
    tile_seg = TILE_ROWS if carry else seg
    nseg = TILE_ROWS // tile_seg
    assert rows % TILE_ROWS == 0 and tile_seg % ROW_BLOCK == 0 and TILE_ROWS % SUB_ROWS == 0
    assert SUB_ROWS % min(tile_seg, SUB_ROWS) == 0 and ROW_BLOCK % NORM_ROWS == 0
    return rows, carry, tile_seg, nseg


def _even_call(x, seg, pos0, layer, consts, h_pool, h_ccv):
    rows, carry, tseg, nseg = _geometry(x, seg)
    nseq = h_pool.shape[0]
    kern = functools.partial(_even_kernel, nseg=nseg, seg=tseg, carry=carry, pos0=pos0)
    return pl.pallas_call(
        kern,
        grid=(rows // TILE_ROWS,),
        in_specs=[_row_spec(D_MODEL)] + [_layer_spec(c, layer) for c in consts]
        + [_state_spec(nseg, POOL_HP, POOL_WIDTH, carry), _state_spec(nseg, CCV_HP, CCV_WIDTH, carry)],
        out_specs=[_row_spec(D_MODEL), _state_spec(nseg, POOL_HP, POOL_WIDTH, carry),
                   _state_spec(nseg, CCV_HP, CCV_WIDTH, carry)],
        out_shape=[jax.ShapeDtypeStruct((rows, D_MODEL), jnp.float32),
                   jax.ShapeDtypeStruct((nseq, POOL_HP, POOL_WIDTH), jnp.float32),
                   jax.ShapeDtypeStruct((nseq, CCV_HP, CCV_WIDTH), jnp.float32)],
        scratch_shapes=[pltpu.VMEM((nseg * (POOL_HP + tseg), POOL_WIDTH), jnp.float32),
                        pltpu.VMEM((nseg * (CCV_HP + tseg), CCV_WIDTH), jnp.float32),
                        pltpu.VMEM((TILE_ROWS, CCV_WIDTH), jnp.float32),
                        pltpu.VMEM((TILE_ROWS, POOL_WIDTH + CCV_WIDTH), jnp.bfloat16)],
        compiler_params=_params(),
        name="even_mixer",
    )(x, *consts, h_pool, h_ccv)


def _odd_call(x, seg, emit_v, layer, consts, h_sc):
    rows, carry, tseg, nseg = _geometry(x, seg)
    nseq = h_sc.shape[0]
    kern = functools.partial(_odd_kernel, nseg=nseg, seg=tseg, carry=carry, emit_v=emit_v)
    out_specs = [_row_spec(D_MODEL), _state_spec(nseg, SHORT_HP, SCONV_WIDTH, carry)]
    out_shape = [jax.ShapeDtypeStruct((rows, D_MODEL), jnp.float32),
                 jax.ShapeDtypeStruct((nseq, SHORT_HP, SCONV_WIDTH), jnp.float32)]
    if emit_v:
        out_specs.append(_row_spec(SGU_WIDTH))
        out_shape.append(jax.ShapeDtypeStruct((rows, SGU_WIDTH), jnp.float32))
    ch = min(tseg, SGU_CHUNK)
    return pl.pallas_call(
        kern,
        grid=(rows // TILE_ROWS,),
        in_specs=[_row_spec(D_MODEL)] + [_layer_spec(c, layer) for c in consts]
        + [_state_spec(nseg, SHORT_HP, SCONV_WIDTH, carry)],
        out_specs=out_specs,
        out_shape=out_shape,
        scratch_shapes=[pltpu.VMEM((TILE_ROWS, SGU_WIDTH + SCONV_WIDTH), jnp.float32),
                        pltpu.VMEM((nseg * (SHORT_HP + tseg), SCONV_WIDTH), jnp.float32),
                        pltpu.VMEM((TILE_ROWS, SGU_WIDTH), jnp.bfloat16),
                        pltpu.VMEM((TILE_ROWS, SGU_WIDTH + SCONV_WIDTH), jnp.bfloat16),
                        pltpu.VMEM((SGU_HEADS, ch, ch), jnp.bfloat16)],
        compiler_params=_params(),
        name="odd_mixer",
    )(x, *consts, h_sc)


def _ffn_call(x, seg, final, layer, consts, g_final, hist):
    rows, carry, tseg, nseg = _geometry(x, seg)
    nseq = hist.shape[0]
    kern = functools.partial(_ffn_kernel, nseg=nseg, seg=tseg, carry=carry, final=final)
    return pl.pallas_call(
        kern,
        grid=(rows // TILE_ROWS,),
        in_specs=[_row_spec(D_MODEL)] + [_layer_spec(c, layer) for c in consts]
        + [_layer_spec(g_final, 0), _state_spec(nseg, SHORT_HP, 2 * D_FF, carry)],
        out_specs=[_row_spec(D_MODEL), _state_spec(nseg, SHORT_HP, 2 * D_FF, carry)],
        out_shape=[jax.ShapeDtypeStruct((rows, D_MODEL), jnp.float32),
                   jax.ShapeDtypeStruct((nseq, SHORT_HP, 2 * D_FF), jnp.float32)],
        scratch_shapes=[pltpu.VMEM((TILE_ROWS, D_MODEL), jnp.bfloat16),
                        pltpu.VMEM((2, 2, nseg * (SHORT_HP + tseg), FF_CHUNK), jnp.float32),
                        pltpu.VMEM((TILE_ROWS, D_FF), jnp.bfloat16)],
        compiler_params=_params(),
        name="conv_ffn",
    )(x, *consts, g_final, hist)


def _pad_hist(hist, hp):
    return jnp.pad(hist, ((0, 0), (hp - hist.shape[1], 0), (0, 0)))


def _trunk(x, seg, pos0, emit_v, st_pool, st_ccv, st_sconv, st_ffn, even, odd, ffn, g_final):
    depth = ffn[1].shape[0]
    new_pool, new_ccv, new_sconv, new_v, new_ffn = [], [], [], [], []
    for i in range(depth):
        j = i // 2
        if i % 2 == 0:
            x, npool, nccv = _even_call(x, seg, pos0, j, even,
                                        _pad_hist(st_pool[j], POOL_HP), _pad_hist(st_ccv[j], CCV_HP))
            new_pool.append(npool[:, POOL_HP - st_pool.shape[2]:])
            new_ccv.append(nccv[:, CCV_HP - st_ccv.shape[2]:])
        else:
            res = _odd_call(x, seg, emit_v, j, odd, _pad_hist(st_sconv[j], SHORT_HP))
            x = res[0]
            new_sconv.append(res[1][:, SHORT_HP - st_sconv.shape[2]:])
            if emit_v:
                new_v.append(res[2])
        x, nffn = _ffn_call(x, seg, i == depth - 1, i, ffn, g_final, _pad_hist(st_ffn[i], SHORT_HP))
        new_ffn.append(nffn[:, SHORT_HP - st_ffn.shape[2]:])
    return x, new_pool, new_ccv, new_sconv, new_v, new_ffn


def kernel(x_prompt, x_sample, state_pool, state_ccv, state_sconv, state_ffn_conv, norm_mix_g, norm_ffn_g, norm_final_g, w_in_even, pool_w, pool_scale, ccv_w, ccv_b, ccv_ln_g, ccv_ln_b, w_out_even, w_in_odd, sgu_ln_g, sgu_ln_b, sgu_ws, sgu_b, sconv_w, w_out_odd, ffn_w_up, ffn_conv_w, ffn_conv_b, ffn_w_down):
    bf16 = jnp.bfloat16
    row = lambda a: a[:, None, :]
    even = (row(norm_mix_g[0::2]), w_in_even.astype(bf16), pool_w.astype(bf16), row(pool_scale), ccv_w, row(ccv_b),
            row(ccv_ln_g), row(ccv_ln_b), w_out_even.astype(bf16))
    odd = (row(norm_mix_g[1::2]), w_in_odd.astype(bf16), row(sgu_ln_g), row(sgu_ln_b), sgu_ws, jnp.swapaxes(sgu_b, 1, 2),
           sconv_w, w_out_odd.astype(bf16))
    ffn = (row(norm_ffn_g), ffn_w_up.astype(bf16), ffn_conv_w, row(ffn_conv_b), ffn_w_down.astype(bf16))
    g_final = norm_final_g[None, None, :]
    n_pair, depth = w_in_even.shape[0], ffn_w_up.shape[0]
    b, seq, _ = x_prompt.shape
    db, dseq, _ = x_sample.shape
    assert b == 1
    f32 = x_prompt.dtype

    zeros = lambda n, k, c: jnp.zeros((n, b, k, c), f32)
    yp, pool_p, ccv_p, sconv_p, _, ffn_p = _trunk(
        x_prompt.reshape(b * seq, D_MODEL), seq, 0, False,
        zeros(n_pair, state_pool.shape[2], POOL_WIDTH), zeros(n_pair, state_ccv.shape[2], CCV_WIDTH),
        zeros(n_pair, state_sconv.shape[2], SCONV_WIDTH), zeros(depth, state_ffn_conv.shape[2], 2 * D_FF),
        even, odd, ffn, g_final)
    ys, pool_s, ccv_s, sconv_s, v_s, ffn_s = _trunk(
        x_sample.reshape(db * dseq, D_MODEL), dseq, PAST_LEN, True,
        state_pool, state_ccv, state_sconv, state_ffn_conv, even, odd, ffn, g_final)
    v_s = [v.reshape(db, dseq, SGU_WIDTH) for v in v_s]
    return (yp.reshape(b, seq, D_MODEL), ys.reshape(db, dseq, D_MODEL),
            jnp.stack(pool_p), jnp.stack(pool_s), jnp.stack(ccv_p), jnp.stack(ccv_s),
            jnp.stack(sconv_p), jnp.stack(sconv_s), jnp.stack(v_s), jnp.stack(ffn_p), jnp.stack(ffn_s))
```

```python
import functools

import jax
import jax.numpy as jnp
from jax import lax
from jax.experimental import pallas as pl
from jax.experimental.pallas import tpu as pltpu

D_MODEL = 1024
PAST_LEN = 4096
POOL_GC = 128
POOL_WINDOWS = (2, 4, 8, 16)
POOL_WIDTH = 512
CCV_WIDTH = 512
CCV_K = 31
SGU_HEADS = 4
SGU_HC = 128
SGU_WIDTH = 512
SGU_CHUNK = 128
SCONV_WIDTH = 512
SHORT_K = 3
D_FF = 2816
EPS = 1e-6

LANES = 128
SUBLANES = 8
TILE_ROWS = 512
TIMES = TILE_ROWS // SUBLANES
SUB_ROWS = 256
ROW_BLOCK = 64
NORM_ROWS = 32
FF_CHUNK = 256
POOL_HP = 16
CCV_HP = 32
SHORT_HP = 8
VMEM_LIMIT_BYTES = 56 * 1024 * 1024


def _rmsnorm_bf16(x, g):
    y = x * lax.rsqrt(jnp.mean(x * x, axis=-1, keepdims=True) + EPS)
    return (y * g).astype(jnp.bfloat16)


def _layernorm(x, g, b):
    mu = jnp.mean(x, axis=-1, keepdims=True)
    xc = x - mu
    var = jnp.mean(xc * xc, axis=-1, keepdims=True)
    return xc * lax.rsqrt(var + EPS) * g + b


def _silu(x):
    return x * jax.nn.sigmoid(x)


def _seed_state(st_ref, hist_ref, carry):
    if carry:
        @pl.when(pl.program_id(0) == 0)
        def _():
            st_ref[...] = hist_ref[...]
    else:
        st_ref[...] = hist_ref[...]


def _ext_row(row, seg, hp):
    s, r = divmod(row, seg)
    return s * (hp + seg) + hp + r


def _store_rows(ext_ref, val, row0, seg, hp):
    n = val.shape[0]
    run = min(seg, n)
    for off in range(0, n, run):
        ext_ref[pl.ds(_ext_row(row0 + off, seg, hp), run), :] = val[off:off + run]


def _causal_conv(win, hp, nrows, taps):
    acc = None
    for r in range(min(len(taps), SUBLANES)):
        rolled = win if r == 0 else pltpu.roll(win, r, axis=0)
        for d in range(r, len(taps), SUBLANES):
            start = hp - (d - r)
            term = taps[d] * rolled[start:start + nrows]
            acc = term if acc is None else acc + term
    return acc


def _load_hist(ext_ref, st_ref, nseg, seg, hp):
    for s in range(nseg):
        ext_ref[pl.ds(s * (hp + seg), hp), :] = st_ref[s]


def _save_hist(ext_ref, st_ref, nseg, seg, hp):
    for s in range(nseg):
        st_ref[s] = ext_ref[pl.ds(s * (hp + seg) + seg, hp), :]


def _pipelined(nsub, stage_in, after_in, stage_mix, stage_out):
    stage_in(0)
    for k in range(nsub):
        if k + 1 < nsub:
            stage_in(k + 1)
        if k == max(nsub - 2, 0):
            after_in()
        stage_mix(k)
        stage_out(k)


def _even_kernel(x_ref, g_ref, win_ref, poolw_ref, pscale_ref, ccvw_ref, ccvb_ref, lng_ref,
                 lnb_ref, wout_ref, hpool_ref, hccv_ref,
                 o_ref, stpool_ref, stccv_ref,
                 extp_ref, extc_ref, cb_ref, cat_ref,
                 *, nseg, seg, carry, pos0):
    tm = nseg * seg
    _seed_state(stpool_ref, hpool_ref, carry)
    _seed_state(stccv_ref, hccv_ref, carry)
    _load_hist(extp_ref, stpool_ref, nseg, seg, POOL_HP)
    _load_hist(extc_ref, stccv_ref, nseg, seg, CCV_HP)
    tile0 = pl.program_id(0) * tm if carry else 0

    def stage_in(k):
        r0 = k * SUB_ROWS
        h = _rmsnorm_bf16(x_ref[r0:r0 + SUB_ROWS, :], g_ref[...])
        proj = jnp.dot(h, win_ref[...], preferred_element_type=jnp.float32)
        _store_rows(extp_ref, proj[:, :POOL_WIDTH], r0, seg, POOL_HP)
        a = proj[:, POOL_WIDTH:POOL_WIDTH + CCV_WIDTH]
        gt = proj[:, POOL_WIDTH + CCV_WIDTH:]
        _store_rows(extc_ref, a * jax.nn.sigmoid(gt), r0, seg, CCV_HP)

    def after_in():
        _save_hist(extp_ref, stpool_ref, nseg, seg, POOL_HP)
        _save_hist(extc_ref, stccv_ref, nseg, seg, CCV_HP)

    def stage_mix(k):
        for b in range(SUB_ROWS // ROW_BLOCK):
            row = k * SUB_ROWS + b * ROW_BLOCK
            rows = slice(row, row + ROW_BLOCK)
            in_seq = row % seg
            prow = _ext_row(row, seg, POOL_HP)
            crow = _ext_row(row, seg, CCV_HP)
            for g, w in enumerate(POOL_WINDOWS):
                lanes = slice(g * POOL_GC, (g + 1) * POOL_GC)
                z = extp_ref[prow - POOL_HP:prow + ROW_BLOCK, lanes]
                s, span = z, 1
                while span < min(w, SUBLANES):
                    s = s + pltpu.roll(s, span, axis=0)
                    span *= 2
                win = s[POOL_HP:]
                for back in range(span, w, span):
                    win = win + s[POOL_HP - back:POOL_HP - back + ROW_BLOCK]
                if pos0 + in_seq >= w - 1:
                    mean = win * (1.0 / w)
                else:
                    pos = pos0 + tile0 + in_seq + lax.broadcasted_iota(jnp.int32, (ROW_BLOCK, POOL_GC), 0)
                    mean = win / jnp.minimum(pos + 1, w).astype(jnp.float32)
                cat_ref[rows, lanes] = (mean - z[POOL_HP:]).astype(jnp.bfloat16)
            for l in range(CCV_WIDTH // LANES):
                lanes = slice(l * LANES, (l + 1) * LANES)
                win = extc_ref[crow - CCV_HP:crow + ROW_BLOCK, lanes]
                taps = [ccvw_ref[CCV_K - 1 - d:CCV_K - d, lanes] for d in range(CCV_K)]
                cb_ref[rows, lanes] = _causal_conv(win, CCV_HP, ROW_BLOCK, taps) + ccvb_ref[:, lanes]
            for n in range(row, row + ROW_BLOCK, NORM_ROWS):
                yb = _silu(_layernorm(cb_ref[n:n + NORM_ROWS, :], lng_ref[...], lnb_ref[...]))
                cat_ref[n:n + NORM_ROWS, POOL_WIDTH:] = yb.astype(jnp.bfloat16)

    def stage_out(k):
        rows = slice(k * SUB_ROWS, (k + 1) * SUB_ROWS)
        for g in range(len(POOL_WINDOWS)):
            lanes = slice(g * POOL_GC, (g + 1) * POOL_GC)
            ya = jnp.dot(cat_ref[rows, lanes], poolw_ref[g], preferred_element_type=jnp.float32)
            cat_ref[rows, lanes] = (ya * pscale_ref[:, lanes]).astype(jnp.bfloat16)
        y = jnp.dot(cat_ref[rows, :], wout_ref[...], preferred_element_type=jnp.float32)
        o_ref[rows, :] = x_ref[rows, :] + y

    _pipelined(tm // SUB_ROWS, stage_in, after_in, stage_mix, stage_out)


def _odd_kernel(x_ref, g_ref, win_ref, lng_ref, lnb_ref, ws_ref, sb_ref, scw_ref, wout_ref, hsc_ref,
                o_ref, stsc_ref, *rest, nseg, seg, carry, emit_v):
    if emit_v:
        v_ref, ug_ref, exts_ref, vb_ref, cat_ref, wsm_ref = rest
    else:
        v_ref = None
        ug_ref, exts_ref, vb_ref, cat_ref, wsm_ref = rest
    tm = nseg * seg
    w = SGU_WIDTH
    _seed_state(stsc_ref, hsc_ref, carry)
    _load_hist(exts_ref, stsc_ref, nseg, seg, SHORT_HP)

    ch = min(seg, SGU_CHUNK)
    tri = (lax.broadcasted_iota(jnp.int32, (ch, ch), 0) >= lax.broadcasted_iota(jnp.int32, (ch, ch), 1))
    for hd in range(SGU_HEADS):
        wsm_ref[hd] = jnp.where(tri, ws_ref[hd, 0:ch, 0:ch], 0.0).astype(jnp.bfloat16)

    def stage_in(k):
        r0 = k * SUB_ROWS
        h = _rmsnorm_bf16(x_ref[r0:r0 + SUB_ROWS, :], g_ref[...])
        proj = jnp.dot(h, win_ref[...], preferred_element_type=jnp.float32)
        ug_ref[r0:r0 + SUB_ROWS, 0:w] = proj[:, 0:w]
        ug_ref[r0:r0 + SUB_ROWS, w:] = proj[:, 2 * w:3 * w]
        for n in range(0, SUB_ROWS, NORM_ROWS):
            v = _layernorm(proj[n:n + NORM_ROWS, w:2 * w], lng_ref[...], lnb_ref[...])
            if emit_v:
                v_ref[r0 + n:r0 + n + NORM_ROWS, :] = v
            vb_ref[r0 + n:r0 + n + NORM_ROWS, :] = v.astype(jnp.bfloat16)
        _store_rows(exts_ref, proj[:, 3 * w:4 * w] * proj[:, 4 * w:], r0, seg, SHORT_HP)

    def after_in():
        _save_hist(exts_ref, stsc_ref, nseg, seg, SHORT_HP)

    def stage_mix(k):
        for hd in range(SGU_HEADS):
            lanes = slice(hd * SGU_HC, (hd + 1) * SGU_HC)
            bias = jnp.broadcast_to(sb_ref[0:ch, hd:hd + 1], (ch, SGU_HC))
            for row in range(k * SUB_ROWS, (k + 1) * SUB_ROWS, ch):
                rows = slice(row, row + ch)
                mixed = jnp.dot(wsm_ref[hd], vb_ref[rows, lanes], preferred_element_type=jnp.float32) + bias
                cat_ref[rows, lanes] = (ug_ref[rows, lanes] * mixed).astype(jnp.bfloat16)
        for row in range(k * SUB_ROWS, (k + 1) * SUB_ROWS, ROW_BLOCK):
            rows = slice(row, row + ROW_BLOCK)
            erow = _ext_row(row, seg, SHORT_HP)
            for l in range(SCONV_WIDTH // LANES):
                lanes = slice(l * LANES, (l + 1) * LANES)
                win = exts_ref[erow - SHORT_HP:erow + ROW_BLOCK, lanes]
                taps = [scw_ref[SHORT_K - 1 - d:SHORT_K - d, lanes] for d in range(SHORT_K)]
                cz = _causal_conv(win, SHORT_HP, ROW_BLOCK, taps)
                cat_ref[rows, w + l * LANES:w + (l + 1) * LANES] = (
                    ug_ref[rows, w + l * LANES:w + (l + 1) * LANES] * cz).astype(jnp.bfloat16)

    def stage_out(k):
        rows = slice(k * SUB_ROWS, (k + 1) * SUB_ROWS)
        y = jnp.dot(cat_ref[rows, :], wout_ref[...], preferred_element_type=jnp.float32)
        o_ref[rows, :] = x_ref[rows, :] + y

    _pipelined(tm // SUB_ROWS, stage_in, after_in, stage_mix, stage_out)


def _ffn_kernel(x_ref, g_ref, wup_ref, cw_ref, cb_ref, wdown_ref, gfin_ref, hist_ref,
                o_ref, st_ref, xp_ref, h_ref, ext_ref, act_ref, *, carry, final):
    nchunk = D_FF // FF_CHUNK
    hb = (SHORT_K - 1) * SUBLANES
    half_rows = TILE_ROWS // 2
    half_times = TIMES // 2
    _seed_state(st_ref, hist_ref, carry)

    for hf in range(2):
        t0 = hf * half_times
        xs = jnp.stack([x_ref[a * TIMES + t0:a * TIMES + t0 + half_times, :] for a in range(SUBLANES)])
        xp = pltpu.einshape("abc->bac", xs).reshape(half_rows, D_MODEL)
        xp_ref[hf * half_rows:(hf + 1) * half_rows, :] = xp
        h_ref[hf * half_rows:(hf + 1) * half_rows, :] = _rmsnorm_bf16(xp, g_ref[...])

    def up_chunk(j):
        for half in range(2):
            col = half * D_FF + j * FF_CHUNK
            cols = slice(col, col + FF_CHUNK)
            buf = ext_ref.at[j % 2, half]
            if j == 0:
                up = jnp.concatenate(
                    [jnp.dot(h_ref[hf * half_rows:(hf + 1) * half_rows, :], wup_ref[:, cols],
                             preferred_element_type=jnp.float32) for hf in range(2)], axis=0)
            else:
                up = jnp.dot(h_ref[...], wup_ref[:, cols], preferred_element_type=jnp.float32)
            buf[hb:, :] = up
            for d in range(1, SHORT_K):
                last = up[(TIMES - d) * SUBLANES:(TIMES - d + 1) * SUBLANES]
                if carry:
                    first = lax.broadcasted_iota(jnp.int32, last.shape, 0) == 0
                    prev = jnp.where(first, st_ref[SHORT_K - 1 - d, :, cols], pltpu.roll(last, 1, axis=0))
                    st_ref[SHORT_K - 1 - d, :, cols] = last[SUBLANES - 1:]
                else:
                    prev = st_ref[SHORT_K - 1 - d, :, cols]
                    st_ref[SHORT_K - 1 - d, :, cols] = last
                buf[hb - d * SUBLANES:hb - (d - 1) * SUBLANES, :] = prev

    def act_chunk(j):
        for row in range(0, TILE_ROWS, ROW_BLOCK):
            cs = []
            for half in range(2):
                cols = slice(half * D_FF + j * FF_CHUNK, half * D_FF + (j + 1) * FF_CHUNK)
                acc = cb_ref[:, cols]
                for d in range(SHORT_K):
                    start = hb + row - d * SUBLANES
                    acc = acc + (cw_ref[SHORT_K - 1 - d:SHORT_K - d, cols]
                                 * ext_ref[j % 2, half, start:start + ROW_BLOCK, :])
                cs.append(acc)
            a = _silu(cs[0]) * cs[1]
            act_ref[row:row + ROW_BLOCK, j * FF_CHUNK:(j + 1) * FF_CHUNK] = a.astype(jnp.bfloat16)

    up_chunk(0)
    for j in range(1, nchunk):
        up_chunk(j)
        act_chunk(j - 1)
    split = (nchunk - 2) * FF_CHUNK
    ys = [jnp.dot(act_ref[hf * half_rows:(hf + 1) * half_rows, :split], wdown_ref[:split, :],
                  preferred_element_type=jnp.float32) for hf in range(2)]
    act_chunk(nchunk - 1)
    for hf in range(2):
        rows = slice(hf * half_rows, (hf + 1) * half_rows)
        y = ys[hf] + jnp.dot(act_ref[rows, split:], wdown_ref[split:, :], preferred_element_type=jnp.float32)
        y = xp_ref[rows, :] + y
        if final:
            y = y * lax.rsqrt(jnp.mean(y * y, axis=-1, keepdims=True) + EPS) * gfin_ref[...]
        y = pltpu.einshape("bac->abc", y.reshape(half_times, SUBLANES, D_MODEL))
        for a in range(SUBLANES):
            t0 = a * TIMES + hf * half_times
            o_ref[t0:t0 + half_times, :] = y[a]


def _layer_spec(arr, layer):
    zeros = (0,) * (arr.ndim - 1)
    return pl.BlockSpec((None,) + arr.shape[1:], lambda i: (layer,) + zeros, pipeline_mode=pl.Buffered(1))


def _row_spec(width):
    return pl.BlockSpec((TILE_ROWS, width), lambda i: (i, 0))


def _state_spec(nseg, hp, width, carry):
    if carry:
        return pl.BlockSpec((nseg, hp, width), lambda i: (0, 0, 0))
    return pl.BlockSpec((nseg, hp, width), lambda i: (i, 0, 0))


def _params():
    return pltpu.CompilerParams(dimension_semantics=("arbitrary",), vmem_limit_bytes=VMEM_LIMIT_BYTES)


def _geometry(x, seg):
    rows = x.shape[0]
    carry = seg > TILE_ROWS
    tile_seg = TILE_ROWS if carry else seg
    nseg = TILE_ROWS // tile_seg
    assert rows % TILE_ROWS == 0 and tile_seg % ROW_BLOCK == 0 and TILE_ROWS % SUB_ROWS == 0
    assert SUB_ROWS % min(tile_seg, SUB_ROWS) == 0 and ROW_BLOCK % NORM_ROWS == 0
    return rows, carry, tile_seg, nseg


def _even_call(x, seg, pos0, layer, consts, h_pool, h_ccv):
    rows, carry, tseg, nseg = _geometry(x, seg)
    nseq = h_pool.shape[0]
    kern = functools.partial(_even_kernel, nseg=nseg, seg=tseg, carry=carry, pos0=pos0)
    return pl.pallas_call(
        kern,
        grid=(rows // TILE_ROWS,),
        in_specs=[_row_spec(D_MODEL)] + [_layer_spec(c, layer) for c in consts]
        + [_state_spec(nseg, POOL_HP, POOL_WIDTH, carry), _state_spec(nseg, CCV_HP, CCV_WIDTH, carry)],
        out_specs=[_row_spec(D_MODEL), _state_spec(nseg, POOL_HP, POOL_WIDTH, carry),
                   _state_spec(nseg, CCV_HP, CCV_WIDTH, carry)],
        out_shape=[jax.ShapeDtypeStruct((rows, D_MODEL), jnp.float32),
                   jax.ShapeDtypeStruct((nseq, POOL_HP, POOL_WIDTH), jnp.float32),
                   jax.ShapeDtypeStruct((nseq, CCV_HP, CCV_WIDTH), jnp.float32)],
        scratch_shapes=[pltpu.VMEM((nseg * (POOL_HP + tseg), POOL_WIDTH), jnp.float32),
                        pltpu.VMEM((nseg * (CCV_HP + tseg), CCV_WIDTH), jnp.float32),
                        pltpu.VMEM((TILE_ROWS, CCV_WIDTH), jnp.float32),
                        pltpu.VMEM((TILE_ROWS, POOL_WIDTH + CCV_WIDTH), jnp.bfloat16)],
        compiler_params=_params(),
        name="even_mixer",
    )(x, *consts, h_pool, h_ccv)


def _odd_call(x, seg, emit_v, layer, consts, h_sc):
    rows, carry, tseg, nseg = _geometry(x, seg)
    nseq = h_sc.shape[0]
    kern = functools.partial(_odd_kernel, nseg=nseg, seg=tseg, carry=carry, emit_v=emit_v)
    out_specs = [_row_spec(D_MODEL), _state_spec(nseg, SHORT_HP, SCONV_WIDTH, carry)]
    out_shape = [jax.ShapeDtypeStruct((rows, D_MODEL), jnp.float32),
                 jax.ShapeDtypeStruct((nseq, SHORT_HP, SCONV_WIDTH), jnp.float32)]
    if emit_v:
        out_specs.append(_row_spec(SGU_WIDTH))
        out_shape.append(jax.ShapeDtypeStruct((rows, SGU_WIDTH), jnp.float32))
    ch = min(tseg, SGU_CHUNK)
    return pl.pallas_call(
        kern,
        grid=(rows // TILE_ROWS,),
        in_specs=[_row_spec(D_MODEL)] + [_layer_spec(c, layer) for c in consts]
        + [_state_spec(nseg, SHORT_HP, SCONV_WIDTH, carry)],
        out_specs=out_specs,
        out_shape=out_shape,
        scratch_shapes=[pltpu.VMEM((TILE_ROWS, SGU_WIDTH + SCONV_WIDTH), jnp.float32),
                        pltpu.VMEM((nseg * (SHORT_HP + tseg), SCONV_WIDTH), jnp.float32),
                        pltpu.VMEM((TILE_ROWS, SGU_WIDTH), jnp.bfloat16),
                        pltpu.VMEM((TILE_ROWS, SGU_WIDTH + SCONV_WIDTH), jnp.bfloat16),
                        pltpu.VMEM((SGU_HEADS, ch, ch), jnp.bfloat16)],
        compiler_params=_params(),
        name="odd_mixer",
    )(x, *consts, h_sc)


def _ffn_call(x, seg, final, layer, consts, g_final, hist):
    rows, carry, tseg, nseg = _geometry(x, seg)
    assert carry or tseg == TIMES
    nseq = hist.shape[2]
    streams = 1 if carry else SUBLANES
    kern = functools.partial(_ffn_kernel, carry=carry, final=final)
    hist_block = (SHORT_K - 1, streams, 2 * D_FF)
    tile_or_first = (lambda i: 0) if carry else (lambda i: i)
    return pl.pallas_call(
        kern,
        grid=(rows // TILE_ROWS,),
        in_specs=[_row_spec(D_MODEL)] + [_layer_spec(c, layer) for c in consts]
        + [_layer_spec(g_final, 0),
           pl.BlockSpec((None,) + hist_block, lambda i: (layer, 0, tile_or_first(i), 0))],
        out_specs=[_row_spec(D_MODEL), pl.BlockSpec(hist_block, lambda i: (0, tile_or_first(i), 0))],
        out_shape=[jax.ShapeDtypeStruct((rows, D_MODEL), jnp.float32),
                   jax.ShapeDtypeStruct((SHORT_K - 1, nseq, 2 * D_FF), jnp.float32)],
        scratch_shapes=[pltpu.VMEM((TILE_ROWS, D_MODEL), jnp.float32),
                        pltpu.VMEM((TILE_ROWS, D_MODEL), jnp.bfloat16),
                        pltpu.VMEM((2, 2, (SHORT_K - 1) * SUBLANES + TILE_ROWS, FF_CHUNK), jnp.float32),
                        pltpu.VMEM((TILE_ROWS, D_FF), jnp.bfloat16)],
        compiler_params=_params(),
        name="conv_ffn",
    )(x, *consts, g_final, hist)


def _pad_hist(hist, hp):
    return jnp.pad(hist, ((0, 0), (hp - hist.shape[1], 0), (0, 0)))


def _trunk(x, seg, pos0, emit_v, st_pool, st_ccv, st_sconv, st_ffn, even, odd, ffn, g_final):
    depth = ffn[1].shape[0]
    new_pool, new_ccv, new_sconv, new_v, new_ffn = [], [], [], [], []
    for i in range(depth):
        j = i // 2
        if i % 2 == 0:
            x, npool, nccv = _even_call(x, seg, pos0, j, even,
                                        _pad_hist(st_pool[j], POOL_HP), _pad_hist(st_ccv[j], CCV_HP))
            new_pool.append(npool[:, POOL_HP - st_pool.shape[2]:])
            new_ccv.append(nccv[:, CCV_HP - st_ccv.shape[2]:])
        else:
            res = _odd_call(x, seg, emit_v, j, odd, _pad_hist(st_sconv[j], SHORT_HP))
            x = res[0]
            new_sconv.append(res[1][:, SHORT_HP - st_sconv.shape[2]:])
            if emit_v:
                new_v.append(res[2])
        x, nffn = _ffn_call(x, seg, i == depth - 1, i, ffn, g_final, st_ffn)
        new_ffn.append(nffn)
    return x, new_pool, new_ccv, new_sconv, new_v, new_ffn


def kernel(x_prompt, x_sample, state_pool, state_ccv, state_sconv, state_ffn_conv, norm_mix_g, norm_ffn_g, norm_final_g, w_in_even, pool_w, pool_scale, ccv_w, ccv_b, ccv_ln_g, ccv_ln_b, w_out_even, w_in_odd, sgu_ln_g, sgu_ln_b, sgu_ws, sgu_b, sconv_w, w_out_odd, ffn_w_up, ffn_conv_w, ffn_conv_b, ffn_w_down):
    bf16 = jnp.bfloat16
    row = lambda a: a[:, None, :]
    even = (row(norm_mix_g[0::2]), w_in_even.astype(bf16), pool_w.astype(bf16), row(pool_scale), ccv_w, row(ccv_b),
            row(ccv_ln_g), row(ccv_ln_b), w_out_even.astype(bf16))
    odd = (row(norm_mix_g[1::2]), w_in_odd.astype(bf16), row(sgu_ln_g), row(sgu_ln_b), sgu_ws, jnp.swapaxes(sgu_b, 1, 2),
           sconv_w, w_out_odd.astype(bf16))
    ffn = (row(norm_ffn_g), ffn_w_up.astype(bf16), ffn_conv_w, row(ffn_conv_b), ffn_w_down.astype(bf16))
    g_final = norm_final_g[None, None, :]
    n_pair, depth = w_in_even.shape[0], ffn_w_up.shape[0]
    b, seq, _ = x_prompt.shape
    db, dseq, _ = x_sample.shape
    assert b == 1
    f32 = x_prompt.dtype

    zeros = lambda n, k, c: jnp.zeros((n, b, k, c), f32)
    yp, pool_p, ccv_p, sconv_p, _, ffn_p = _trunk(
        x_prompt.reshape(b * seq, D_MODEL), seq, 0, False,
        zeros(n_pair, state_pool.shape[2], POOL_WIDTH), zeros(n_pair, state_ccv.shape[2], CCV_WIDTH),
        zeros(n_pair, state_sconv.shape[2], SCONV_WIDTH), jnp.zeros((depth, SHORT_K - 1, b, 2 * D_FF), f32),
        even, odd, ffn, g_final)
    ys, pool_s, ccv_s, sconv_s, v_s, ffn_s = _trunk(
        x_sample.reshape(db * dseq, D_MODEL), dseq, PAST_LEN, True,
        state_pool, state_ccv, state_sconv, jnp.swapaxes(state_ffn_conv, 1, 2), even, odd, ffn, g_final)
    v_s = [v.reshape(db, dseq, SGU_WIDTH) for v in v_s]
    return (yp.reshape(b, seq, D_MODEL), ys.reshape(db, dseq, D_MODEL),
            jnp.stack(pool_p), jnp.stack(pool_s), jnp.stack(ccv_p), jnp.stack(ccv_s),
            jnp.stack(sconv_p), jnp.stack(sconv_s), jnp.stack(v_s),
            jnp.swapaxes(jnp.stack(ffn_p), 1, 2), jnp.swapaxes(jnp.stack(ffn_s), 1, 2))
```

```python
import functools

import jax
import jax.numpy as jnp
from jax import lax
from jax.experimental import pallas as pl
from jax.experimental.pallas import tpu as pltpu

D_MODEL = 1024
PAST_LEN = 4096
POOL_GC = 128
POOL_WINDOWS = (2, 4, 8, 16)
POOL_WIDTH = 512
CCV_WIDTH = 512
CCV_K = 31
SGU_HEADS = 4
SGU_HC = 128
SGU_WIDTH = 512
SGU_CHUNK = 128
SCONV_WIDTH = 512
SHORT_K = 3
D_FF = 2816
EPS = 1e-6

LANES = 128
SUBLANES = 8
STEP_ROWS = 1024
TILE_ROWS = 512
TIMES = TILE_ROWS // SUBLANES
SUB_ROWS = 256
ROW_BLOCK = 64
NORM_ROWS = 32
FF_CHUNK = 256
POOL_HP = 16
CCV_HP = 32
SHORT_HP = 8
VMEM_LIMIT_BYTES = 56 * 1024 * 1024


def _rmsnorm_bf16(x, g):
    y = x * lax.rsqrt(jnp.mean(x * x, axis=-1, keepdims=True) + EPS)
    return (y * g).astype(jnp.bfloat16)


def _layernorm(x, g, b):
    mu = jnp.mean(x, axis=-1, keepdims=True)
    xc = x - mu
    var = jnp.mean(xc * xc, axis=-1, keepdims=True)
    return xc * lax.rsqrt(var + EPS) * g + b


def _silu(x):
    return x * jax.nn.sigmoid(x)


def _seed_state(st_ref, hist_ref, carry):
    if carry:
        @pl.when(pl.program_id(0) == 0)
        def _():
            st_ref[...] = hist_ref[...]
    else:
        st_ref[...] = hist_ref[...]


def _ext_row(row, seg, hp):
    s, r = divmod(row, seg)
    return s * (hp + seg) + hp + r


def _store_rows(ext_ref, val, row0, seg, hp):
    n = val.shape[0]
    run = min(seg, n)
    for off in range(0, n, run):
        ext_ref[pl.ds(_ext_row(row0 + off, seg, hp), run), :] = val[off:off + run]


def _causal_conv(win, hp, nrows, taps):
    acc = None
    for r in range(min(len(taps), SUBLANES)):
        rolled = win if r == 0 else pltpu.roll(win, r, axis=0)
        for d in range(r, len(taps), SUBLANES):
            start = hp - (d - r)
            term = taps[d] * rolled[start:start + nrows]
            acc = term if acc is None else acc + term
    return acc


def _load_hist(ext_ref, st_ref, nseg, seg, hp):
    for s in range(nseg):
        ext_ref[pl.ds(s * (hp + seg), hp), :] = st_ref[s]


def _save_hist(ext_ref, st_ref, nseg, seg, hp):
    for s in range(nseg):
        st_ref[s] = ext_ref[pl.ds(s * (hp + seg) + seg, hp), :]


def _pipelined(nsub, stage_in, after_in, stage_mix, stage_out):
    stage_in(0)
    for k in range(nsub):
        if k + 1 < nsub:
            stage_in(k + 1)
        if k == max(nsub - 2, 0):
            after_in()
        stage_mix(k)
        stage_out(k)


def _even_kernel(x_ref, g_ref, win_ref, poolw_ref, pscale_ref, ccvw_ref, ccvb_ref, lng_ref,
                 lnb_ref, wout_ref, hpool_ref, hccv_ref,
                 o_ref, stpool_ref, stccv_ref,
                 extp_ref, extc_ref, cb_ref, cat_ref,
                 *, nseg, seg, carry, pos0):
    tm = nseg * seg
    _seed_state(stpool_ref, hpool_ref, carry)
    _seed_state(stccv_ref, hccv_ref, carry)
    _load_hist(extp_ref, stpool_ref, nseg, seg, POOL_HP)
    _load_hist(extc_ref, stccv_ref, nseg, seg, CCV_HP)
    tile0 = pl.program_id(0) * tm if carry else 0

    def stage_in(k):
        r0 = k * SUB_ROWS
        h = _rmsnorm_bf16(x_ref[r0:r0 + SUB_ROWS, :], g_ref[...])
        proj = jnp.dot(h, win_ref[...], preferred_element_type=jnp.float32)
        _store_rows(extp_ref, proj[:, :POOL_WIDTH], r0, seg, POOL_HP)
        a = proj[:, POOL_WIDTH:POOL_WIDTH + CCV_WIDTH]
        gt = proj[:, POOL_WIDTH + CCV_WIDTH:]
        _store_rows(extc_ref, a * jax.nn.sigmoid(gt), r0, seg, CCV_HP)

    def after_in():
        _save_hist(extp_ref, stpool_ref, nseg, seg, POOL_HP)
        _save_hist(extc_ref, stccv_ref, nseg, seg, CCV_HP)

    def stage_mix(k):
        for b in range(SUB_ROWS // ROW_BLOCK):
            row = k * SUB_ROWS + b * ROW_BLOCK
            rows = slice(row, row + ROW_BLOCK)
            in_seq = row % seg
            prow = _ext_row(row, seg, POOL_HP)
            crow = _ext_row(row, seg, CCV_HP)
            for g, w in enumerate(POOL_WINDOWS):
                lanes = slice(g * POOL_GC, (g + 1) * POOL_GC)
                z = extp_ref[prow - POOL_HP:prow + ROW_BLOCK, lanes]
                s, span = z, 1
                while span < min(w, SUBLANES):
                    s = s + pltpu.roll(s, span, axis=0)
                    span *= 2
                win = s[POOL_HP:]
                for back in range(span, w, span):
                    win = win + s[POOL_HP - back:POOL_HP - back + ROW_BLOCK]
                if pos0 + in_seq >= w - 1:
                    mean = win * (1.0 / w)
                else:
                    pos = pos0 + tile0 + in_seq + lax.broadcasted_iota(jnp.int32, (ROW_BLOCK, POOL_GC), 0)
                    mean = win / jnp.minimum(pos + 1, w).astype(jnp.float32)
                cat_ref[rows, lanes] = (mean - z[POOL_HP:]).astype(jnp.bfloat16)
            for l in range(CCV_WIDTH // LANES):
                lanes = slice(l * LANES, (l + 1) * LANES)
                win = extc_ref[crow - CCV_HP:crow + ROW_BLOCK, lanes]
                taps = [ccvw_ref[CCV_K - 1 - d:CCV_K - d, lanes] for d in range(CCV_K)]
                cb_ref[rows, lanes] = _causal_conv(win, CCV_HP, ROW_BLOCK, taps) + ccvb_ref[:, lanes]
            for n in range(row, row + ROW_BLOCK, NORM_ROWS):
                yb = _silu(_layernorm(cb_ref[n:n + NORM_ROWS, :], lng_ref[...], lnb_ref[...]))
                cat_ref[n:n + NORM_ROWS, POOL_WIDTH:] = yb.astype(jnp.bfloat16)

    def stage_out(k):
        rows = slice(k * SUB_ROWS, (k + 1) * SUB_ROWS)
        for g in range(len(POOL_WINDOWS)):
            lanes = slice(g * POOL_GC, (g + 1) * POOL_GC)
            ya = jnp.dot(cat_ref[rows, lanes], poolw_ref[g], preferred_element_type=jnp.float32)
            cat_ref[rows, lanes] = (ya * pscale_ref[:, lanes]).astype(jnp.bfloat16)
        y = jnp.dot(cat_ref[rows, :], wout_ref[...], preferred_element_type=jnp.float32)
        o_ref[rows, :] = x_ref[rows, :] + y

    _pipelined(tm // SUB_ROWS, stage_in, after_in, stage_mix, stage_out)


def _odd_kernel(x_ref, g_ref, win_ref, lng_ref, lnb_ref, ws_ref, sb_ref, scw_ref, wout_ref, hsc_ref,
                o_ref, stsc_ref, *rest, nseg, seg, carry, emit_v):
    if emit_v:
        v_ref, ug_ref, exts_ref, vb_ref, cat_ref, wsm_ref = rest
    else:
        v_ref = None
        ug_ref, exts_ref, vb_ref, cat_ref, wsm_ref = rest
    tm = nseg * seg
    w = SGU_WIDTH
    _seed_state(stsc_ref, hsc_ref, carry)
    _load_hist(exts_ref, stsc_ref, nseg, seg, SHORT_HP)

    ch = min(seg, SGU_CHUNK)
    tri = (lax.broadcasted_iota(jnp.int32, (ch, ch), 0) >= lax.broadcasted_iota(jnp.int32, (ch, ch), 1))
    for hd in range(SGU_HEADS):
        wsm_ref[hd] = jnp.where(tri, ws_ref[hd, 0:ch, 0:ch], 0.0).astype(jnp.bfloat16)

    def stage_in(k):
        r0 = k * SUB_ROWS
        h = _rmsnorm_bf16(x_ref[r0:r0 + SUB_ROWS, :], g_ref[...])
        proj = jnp.dot(h, win_ref[...], preferred_element_type=jnp.float32)
        ug_ref[r0:r0 + SUB_ROWS, 0:w] = proj[:, 0:w]
        ug_ref[r0:r0 + SUB_ROWS, w:] = proj[:, 2 * w:3 * w]
        for n in range(0, SUB_ROWS, NORM_ROWS):
            v = _layernorm(proj[n:n + NORM_ROWS, w:2 * w], lng_ref[...], lnb_ref[...])
            if emit_v:
                v_ref[r0 + n:r0 + n + NORM_ROWS, :] = v
            vb_ref[r0 + n:r0 + n + NORM_ROWS, :] = v.astype(jnp.bfloat16)
        _store_rows(exts_ref, proj[:, 3 * w:4 * w] * proj[:, 4 * w:], r0, seg, SHORT_HP)

    def after_in():
        _save_hist(exts_ref, stsc_ref, nseg, seg, SHORT_HP)

    def stage_mix(k):
        for hd in range(SGU_HEADS):
            lanes = slice(hd * SGU_HC, (hd + 1) * SGU_HC)
            bias = jnp.broadcast_to(sb_ref[0:ch, hd:hd + 1], (ch, SGU_HC))
            for row in range(k * SUB_ROWS, (k + 1) * SUB_ROWS, ch):
                rows = slice(row, row + ch)
                mixed = jnp.dot(wsm_ref[hd], vb_ref[rows, lanes], preferred_element_type=jnp.float32) + bias
                cat_ref[rows, lanes] = (ug_ref[rows, lanes] * mixed).astype(jnp.bfloat16)
        for row in range(k * SUB_ROWS, (k + 1) * SUB_ROWS, ROW_BLOCK):
            rows = slice(row, row + ROW_BLOCK)
            erow = _ext_row(row, seg, SHORT_HP)
            for l in range(SCONV_WIDTH // LANES):
                lanes = slice(l * LANES, (l + 1) * LANES)
                win = exts_ref[erow - SHORT_HP:erow + ROW_BLOCK, lanes]
                taps = [scw_ref[SHORT_K - 1 - d:SHORT_K - d, lanes] for d in range(SHORT_K)]
                cz = _causal_conv(win, SHORT_HP, ROW_BLOCK, taps)
                cat_ref[rows, w + l * LANES:w + (l + 1) * LANES] = (
                    ug_ref[rows, w + l * LANES:w + (l + 1) * LANES] * cz).astype(jnp.bfloat16)

    def stage_out(k):
        rows = slice(k * SUB_ROWS, (k + 1) * SUB_ROWS)
        y = jnp.dot(cat_ref[rows, :], wout_ref[...], preferred_element_type=jnp.float32)
        o_ref[rows, :] = x_ref[rows, :] + y

    _pipelined(tm // SUB_ROWS, stage_in, after_in, stage_mix, stage_out)


def _ffn_tile(x_ref, g_ref, wup_ref, cw_ref, cb_ref, wdown_ref, gfin_ref,
              o_ref, st_ref, xp_ref, h_ref, ext_ref, act_ref, *, carry, final):
    nchunk = D_FF // FF_CHUNK
    hb = (SHORT_K - 1) * SUBLANES
    half_rows = TILE_ROWS // 2
    half_times = TIMES // 2

    for hf in range(2):
        t0 = hf * half_times
        xs = jnp.stack([x_ref[a * TIMES + t0:a * TIMES + t0 + half_times, :] for a in range(SUBLANES)])
        xp = pltpu.einshape("abc->bac", xs).reshape(half_rows, D_MODEL)
        xp_ref[hf * half_rows:(hf + 1) * half_rows, :] = xp
        h_ref[hf * half_rows:(hf + 1) * half_rows, :] = _rmsnorm_bf16(xp, g_ref[...])

    def up_chunk(j):
        for half in range(2):
            col = half * D_FF + j * FF_CHUNK
            cols = slice(col, col + FF_CHUNK)
            buf = ext_ref.at[j % 2, half]
            if j == 0:
                up = jnp.concatenate(
                    [jnp.dot(h_ref[hf * half_rows:(hf + 1) * half_rows, :], wup_ref[:, cols],
                             preferred_element_type=jnp.float32) for hf in range(2)], axis=0)
            else:
                up = jnp.dot(h_ref[...], wup_ref[:, cols], preferred_element_type=jnp.float32)
            buf[hb:, :] = up
            for d in range(1, SHORT_K):
                last = up[(TIMES - d) * SUBLANES:(TIMES - d + 1) * SUBLANES]
                if carry:
                    first = lax.broadcasted_iota(jnp.int32, last.shape, 0) == 0
                    prev = jnp.where(first, st_ref[SHORT_K - 1 - d, :, cols], pltpu.roll(last, 1, axis=0))
                    st_ref[SHORT_K - 1 - d, :, cols] = last[SUBLANES - 1:]
                else:
                    prev = st_ref[SHORT_K - 1 - d, :, cols]
                    st_ref[SHORT_K - 1 - d, :, cols] = last
                buf[hb - d * SUBLANES:hb - (d - 1) * SUBLANES, :] = prev

    def act_chunk(j):
        for row in range(0, TILE_ROWS, ROW_BLOCK):
            cs = []
            for half in range(2):
                cols = slice(half * D_FF + j * FF_CHUNK, half * D_FF + (j + 1) * FF_CHUNK)
                acc = cb_ref[:, cols]
                for d in range(SHORT_K):
                    start = hb + row - d * SUBLANES
                    acc = acc + (cw_ref[SHORT_K - 1 - d:SHORT_K - d, cols]
                                 * ext_ref[j % 2, half, start:start + ROW_BLOCK, :])
                cs.append(acc)
            a = _silu(cs[0]) * cs[1]
            act_ref[row:row + ROW_BLOCK, j * FF_CHUNK:(j + 1) * FF_CHUNK] = a.astype(jnp.bfloat16)

    up_chunk(0)
    for j in range(1, nchunk):
        up_chunk(j)
        act_chunk(j - 1)
    split = (nchunk - 2) * FF_CHUNK
    ys = [jnp.dot(act_ref[hf * half_rows:(hf + 1) * half_rows, :split], wdown_ref[:split, :],
                  preferred_element_type=jnp.float32) for hf in range(2)]
    act_chunk(nchunk - 1)
    for hf in range(2):
        rows = slice(hf * half_rows, (hf + 1) * half_rows)
        y = ys[hf] + jnp.dot(act_ref[rows, split:], wdown_ref[split:, :], preferred_element_type=jnp.float32)
        y = xp_ref[rows, :] + y
        if final:
            y = y * lax.rsqrt(jnp.mean(y * y, axis=-1, keepdims=True) + EPS) * gfin_ref[...]
        y = pltpu.einshape("bac->abc", y.reshape(half_times, SUBLANES, D_MODEL))
        for a in range(SUBLANES):
            t0 = a * TIMES + hf * half_times
            o_ref[t0:t0 + half_times, :] = y[a]


def _ffn_kernel(x_ref, g_ref, wup_ref, cw_ref, cb_ref, wdown_ref, gfin_ref, hist_ref,
                o_ref, st_ref, *scratch, carry, final):
    _seed_state(st_ref, hist_ref, carry)
    for p in range(STEP_ROWS // TILE_ROWS):
        rows = pl.ds(p * TILE_ROWS, TILE_ROWS)
        st = st_ref if carry else st_ref.at[:, pl.ds(p * SUBLANES, SUBLANES), :]
        _ffn_tile(x_ref.at[rows, :], g_ref, wup_ref, cw_ref, cb_ref, wdown_ref, gfin_ref,
                  o_ref.at[rows, :], st, *scratch[4 * p:4 * p + 4], carry=carry, final=final)


def _layer_spec(arr, layer):
    zeros = (0,) * (arr.ndim - 1)
    return pl.BlockSpec((None,) + arr.shape[1:], lambda i: (layer,) + zeros, pipeline_mode=pl.Buffered(1))


def _row_spec(width):
    return pl.BlockSpec((STEP_ROWS, width), lambda i: (i, 0))


def _state_spec(nseg, hp, width, carry):
    if carry:
        return pl.BlockSpec((nseg, hp, width), lambda i: (0, 0, 0))
    return pl.BlockSpec((nseg, hp, width), lambda i: (i, 0, 0))


def _params():
    return pltpu.CompilerParams(dimension_semantics=("arbitrary",), vmem_limit_bytes=VMEM_LIMIT_BYTES)


def _geometry(x, seg):
    rows = x.shape[0]
    carry = seg > STEP_ROWS
    tile_seg = STEP_ROWS if carry else seg
    nseg = STEP_ROWS // tile_seg
    assert rows % STEP_ROWS == 0 and tile_seg % ROW_BLOCK == 0 and STEP_ROWS % SUB_ROWS == 0
    assert SUB_ROWS % min(tile_seg, SUB_ROWS) == 0 and ROW_BLOCK % NORM_ROWS == 0
    return rows, carry, tile_seg, nseg


def _even_call(x, seg, pos0, layer, consts, h_pool, h_ccv):
    rows, carry, tseg, nseg = _geometry(x, seg)
    nseq = h_pool.shape[0]
    kern = functools.partial(_even_kernel, nseg=nseg, seg=tseg, carry=carry, pos0=pos0)
    return pl.pallas_call(
        kern,
        grid=(rows // STEP_ROWS,),
        in_specs=[_row_spec(D_MODEL)] + [_layer_spec(c, layer) for c in consts]
        + [_state_spec(nseg, POOL_HP, POOL_WIDTH, carry), _state_spec(nseg, CCV_HP, CCV_WIDTH, carry)],
        out_specs=[_row_spec(D_MODEL), _state_spec(nseg, POOL_HP, POOL_WIDTH, carry),
                   _state_spec(nseg, CCV_HP, CCV_WIDTH, carry)],
        out_shape=[jax.ShapeDtypeStruct((rows, D_MODEL), jnp.float32),
                   jax.ShapeDtypeStruct((nseq, POOL_HP, POOL_WIDTH), jnp.float32),
                   jax.ShapeDtypeStruct((nseq, CCV_HP, CCV_WIDTH), jnp.float32)],
        scratch_shapes=[pltpu.VMEM((nseg * (POOL_HP + tseg), POOL_WIDTH), jnp.float32),
                        pltpu.VMEM((nseg * (CCV_HP + tseg), CCV_WIDTH), jnp.float32),
                        pltpu.VMEM((STEP_ROWS, CCV_WIDTH), jnp.float32),
                        pltpu.VMEM((STEP_ROWS, POOL_WIDTH + CCV_WIDTH), jnp.bfloat16)],
        compiler_params=_params(),
        name="even_mixer",
    )(x, *consts, h_pool, h_ccv)


def _odd_call(x, seg, emit_v, layer, consts, h_sc):
    rows, carry, tseg, nseg = _geometry(x, seg)
    nseq = h_sc.shape[0]
    kern = functools.partial(_odd_kernel, nseg=nseg, seg=tseg, carry=carry, emit_v=emit_v)
    out_specs = [_row_spec(D_MODEL), _state_spec(nseg, SHORT_HP, SCONV_WIDTH, carry)]
    out_shape = [jax.ShapeDtypeStruct((rows, D_MODEL), jnp.float32),
                 jax.ShapeDtypeStruct((nseq, SHORT_HP, SCONV_WIDTH), jnp.float32)]
    if emit_v:
        out_specs.append(_row_spec(SGU_WIDTH))
        out_shape.append(jax.ShapeDtypeStruct((rows, SGU_WIDTH), jnp.float32))
    ch = min(tseg, SGU_CHUNK)
    return pl.pallas_call(
        kern,
        grid=(rows // STEP_ROWS,),
        in_specs=[_row_spec(D_MODEL)] + [_layer_spec(c, layer) for c in consts]
        + [_state_spec(nseg, SHORT_HP, SCONV_WIDTH, carry)],
        out_specs=out_specs,
        out_shape=out_shape,
        scratch_shapes=[pltpu.VMEM((STEP_ROWS, SGU_WIDTH + SCONV_WIDTH), jnp.float32),
                        pltpu.VMEM((nseg * (SHORT_HP + tseg), SCONV_WIDTH), jnp.float32),
                        pltpu.VMEM((STEP_ROWS, SGU_WIDTH), jnp.bfloat16),
                        pltpu.VMEM((STEP_ROWS, SGU_WIDTH + SCONV_WIDTH), jnp.bfloat16),
                        pltpu.VMEM((SGU_HEADS, ch, ch), jnp.bfloat16)],
        compiler_params=_params(),
        name="odd_mixer",
    )(x, *consts, h_sc)


def _ffn_call(x, seg, final, layer, consts, g_final, hist):
    rows, carry, tseg, nseg = _geometry(x, seg)
    assert carry or tseg == TIMES
    nseq = hist.shape[2]
    streams = 1 if carry else STEP_ROWS // TIMES
    kern = functools.partial(_ffn_kernel, carry=carry, final=final)
    hist_block = (SHORT_K - 1, streams, 2 * D_FF)
    tile_or_first = (lambda i: 0) if carry else (lambda i: i)
    return pl.pallas_call(
        kern,
        grid=(rows // STEP_ROWS,),
        in_specs=[_row_spec(D_MODEL)] + [_layer_spec(c, layer) for c in consts]
        + [_layer_spec(g_final, 0),
           pl.BlockSpec((None,) + hist_block, lambda i: (layer, 0, tile_or_first(i), 0))],
        out_specs=[_row_spec(D_MODEL), pl.BlockSpec(hist_block, lambda i: (0, tile_or_first(i), 0))],
        out_shape=[jax.ShapeDtypeStruct((rows, D_MODEL), jnp.float32),
                   jax.ShapeDtypeStruct((SHORT_K - 1, nseq, 2 * D_FF), jnp.float32)],
        scratch_shapes=[pltpu.VMEM((TILE_ROWS, D_MODEL), jnp.float32),
                        pltpu.VMEM((TILE_ROWS, D_MODEL), jnp.bfloat16),
                        pltpu.VMEM((2, 2, (SHORT_K - 1) * SUBLANES + TILE_ROWS, FF_CHUNK), jnp.float32),
                        pltpu.VMEM((TILE_ROWS, D_FF), jnp.bfloat16)] * (STEP_ROWS // TILE_ROWS),
        compiler_params=_params(),
        name="conv_ffn",
    )(x, *consts, g_final, hist)


def _pad_hist(hist, hp):
    return jnp.pad(hist, ((0, 0), (hp - hist.shape[1], 0), (0, 0)))


def _trunk(x, seg, pos0, emit_v, st_pool, st_ccv, st_sconv, st_ffn, even, odd, ffn, g_final):
    depth = ffn[1].shape[0]
    new_pool, new_ccv, new_sconv, new_v, new_ffn = [], [], [], [], []
    for i in range(depth):
        j = i // 2
        if i % 2 == 0:
            x, npool, nccv = _even_call(x, seg, pos0, j, even,
                                        _pad_hist(st_pool[j], POOL_HP), _pad_hist(st_ccv[j], CCV_HP))
            new_pool.append(npool[:, POOL_HP - st_pool.shape[2]:])
            new_ccv.append(nccv[:, CCV_HP - st_ccv.shape[2]:])
        else:
            res = _odd_call(x, seg, emit_v, j, odd, _pad_hist(st_sconv[j], SHORT_HP))
            x = res[0]
            new_sconv.append(res[1][:, SHORT_HP - st_sconv.shape[2]:])
            if emit_v:
                new_v.append(res[2])
        x, nffn = _ffn_call(x, seg, i == depth - 1, i, ffn, g_final, st_ffn)
        new_ffn.append(nffn)
    return x, new_pool, new_ccv, new_sconv, new_v, new_ffn


def kernel(x_prompt, x_sample, state_pool, state_ccv, state_sconv, state_ffn_conv, norm_mix_g, norm_ffn_g, norm_final_g, w_in_even, pool_w, pool_scale, ccv_w, ccv_b, ccv_ln_g, ccv_ln_b, w_out_even, w_in_odd, sgu_ln_g, sgu_ln_b, sgu_ws, sgu_b, sconv_w, w_out_odd, ffn_w_up, ffn_conv_w, ffn_conv_b, ffn_w_down):
    bf16 = jnp.bfloat16
    row = lambda a: a[:, None, :]
    even = (row(norm_mix_g[0::2]), w_in_even.astype(bf16), pool_w.astype(bf16), row(pool_scale), ccv_w, row(ccv_b),
            row(ccv_ln_g), row(ccv_ln_b), w_out_even.astype(bf16))
    odd = (row(norm_mix_g[1::2]), w_in_odd.astype(bf16), row(sgu_ln_g), row(sgu_ln_b), sgu_ws, jnp.swapaxes(sgu_b, 1, 2),
           sconv_w, w_out_odd.astype(bf16))
    ffn = (row(norm_ffn_g), ffn_w_up.astype(bf16), ffn_conv_w, row(ffn_conv_b), ffn_w_down.astype(bf16))
    g_final = norm_final_g[None, None, :]
    n_pair, depth = w_in_even.shape[0], ffn_w_up.shape[0]
    b, seq, _ = x_prompt.shape
    db, dseq, _ = x_sample.shape
    assert b == 1
    f32 = x_prompt.dtype

    zeros = lambda n, k, c: jnp.zeros((n, b, k, c), f32)
    yp, pool_p, ccv_p, sconv_p, _, ffn_p = _trunk(
        x_prompt.reshape(b * seq, D_MODEL), seq, 0, False,
        zeros(n_pair, state_pool.shape[2], POOL_WIDTH), zeros(n_pair, state_ccv.shape[2], CCV_WIDTH),
        zeros(n_pair, state_sconv.shape[2], SCONV_WIDTH), jnp.zeros((depth, SHORT_K - 1, b, 2 * D_FF), f32),
        even, odd, ffn, g_final)
    ys, pool_s, ccv_s, sconv_s, v_s, ffn_s = _trunk(
        x_sample.reshape(db * dseq, D_MODEL), dseq, PAST_LEN, True,
        state_pool, state_ccv, state_sconv, jnp.swapaxes(state_ffn_conv, 1, 2), even, odd, ffn, g_final)
    v_s = [v.reshape(db, dseq, SGU_WIDTH) for v in v_s]
    return (yp.reshape(b, seq, D_MODEL), ys.reshape(db, dseq, D_MODEL),
            jnp.stack(pool_p), jnp.stack(pool_s), jnp.stack(ccv_p), jnp.stack(ccv_s),
            jnp.stack(sconv_p), jnp.stack(sconv_s), jnp.stack(v_s),
            jnp.swapaxes(jnp.stack(ffn_p), 1, 2), jnp.swapaxes(jnp.stack(ffn_s), 1, 2))
```

```python
import functools

import jax
import jax.numpy as jnp
from jax import lax
from jax.experimental import pallas as pl
from jax.experimental.pallas import tpu as pltpu

D_MODEL = 1024
PAST_LEN = 4096
POOL_GC = 128
POOL_WINDOWS = (2, 4, 8, 16)
POOL_WIDTH = 512
CCV_WIDTH = 512
CCV_K = 31
SGU_HEADS = 4
SGU_HC = 128
SGU_WIDTH = 512
SGU_CHUNK = 128
SCONV_WIDTH = 512
SHORT_K = 3
D_FF = 2816
EPS = 1e-6

LANES = 128
SUBLANES = 8
TILE_ROWS = 512
TIMES = TILE_ROWS // SUBLANES
SUB_ROWS = 256
ROW_BLOCK = 64
NORM_ROWS = 32
FF_CHUNK = 256
POOL_HP = 16
CCV_HP = 32
SHORT_HP = 8
VMEM_LIMIT_BYTES = 56 * 1024 * 1024


def _rmsnorm_bf16(x, g):
    y = x * lax.rsqrt(jnp.mean(x * x, axis=-1, keepdims=True) + EPS)
    return (y * g).astype(jnp.bfloat16)


def _layernorm(x, g, b):
    mu = jnp.mean(x, axis=-1, keepdims=True)
    xc = x - mu
    var = jnp.mean(xc * xc, axis=-1, keepdims=True)
    return xc * lax.rsqrt(var + EPS) * g + b


def _silu(x):
    return x * jax.nn.sigmoid(x)


def _seed_state(st_ref, hist_ref, carry):
    if carry:
        @pl.when(pl.program_id(0) == 0)
        def _():
            st_ref[...] = hist_ref[...]
    else:
        st_ref[...] = hist_ref[...]


def _ext_row(row, seg, hp):
    s, r = divmod(row, seg)
    return s * (hp + seg) + hp + r


def _store_rows(ext_ref, val, row0, seg, hp):
    n = val.shape[0]
    run = min(seg, n)
    for off in range(0, n, run):
        ext_ref[pl.ds(_ext_row(row0 + off, seg, hp), run), :] = val[off:off + run]


def _causal_conv(win, hp, nrows, taps):
    acc = None
    for r in range(min(len(taps), SUBLANES)):
        rolled = win if r == 0 else pltpu.roll(win, r, axis=0)
        for d in range(r, len(taps), SUBLANES):
            start = hp - (d - r)
            term = taps[d] * rolled[start:start + nrows]
            acc = term if acc is None else acc + term
    return acc


def _load_hist(ext_ref, st_ref, nseg, seg, hp):
    for s in range(nseg):
        ext_ref[pl.ds(s * (hp + seg), hp), :] = st_ref[s]


def _save_hist(ext_ref, st_ref, nseg, seg, hp):
    for s in range(nseg):
        st_ref[s] = ext_ref[pl.ds(s * (hp + seg) + seg, hp), :]


def _pipelined(nsub, stage_in, after_in, stage_mix, stage_out):
    stage_in(0)
    for k in range(nsub):
        if k + 1 < nsub:
            stage_in(k + 1)
        if k == max(nsub - 2, 0):
            after_in()
        stage_mix(k)
        stage_out(k)


def _even_kernel(x_ref, g_ref, win_ref, poolw_ref, pscale_ref, ccvw_ref, ccvb_ref, lng_ref,
                 lnb_ref, wout_ref, hpool_ref, hccv_ref,
                 o_ref, stpool_ref, stccv_ref,
                 h_ref, extp_ref, extc_ref, cb_ref, cat_ref,
                 *, nseg, seg, carry, pos0):
    tm = nseg * seg
    _seed_state(stpool_ref, hpool_ref, carry)
    _seed_state(stccv_ref, hccv_ref, carry)
    _load_hist(extp_ref, stpool_ref, nseg, seg, POOL_HP)
    _load_hist(extc_ref, stccv_ref, nseg, seg, CCV_HP)
    tile0 = pl.program_id(0) * tm if carry else 0
    chunk = 2 * LANES

    def store_rows(ext_ref, val, hp, cols):
        for s0 in range(nseg):
            ext_ref[pl.ds(s0 * (hp + seg) + hp, seg), cols] = val[s0 * seg:(s0 + 1) * seg]

    def stage_in(c):
        cols = slice(c * chunk, (c + 1) * chunk)
        proj = [jnp.dot(h_ref[...], win_ref[:, p * POOL_WIDTH + c * chunk:p * POOL_WIDTH + (c + 1) * chunk],
                        preferred_element_type=jnp.float32) for p in range(3)]
        store_rows(extp_ref, proj[0], POOL_HP, cols)
        store_rows(extc_ref, proj[1] * jax.nn.sigmoid(proj[2]), CCV_HP, cols)

    def mix_chunk(c):
        for row in range(0, tm, ROW_BLOCK):
            rows = slice(row, row + ROW_BLOCK)
            in_seq = row % seg
            prow = _ext_row(row, seg, POOL_HP)
            crow = _ext_row(row, seg, CCV_HP)
            for g in range(c * chunk // POOL_GC, (c + 1) * chunk // POOL_GC):
                w = POOL_WINDOWS[g]
                lanes = slice(g * POOL_GC, (g + 1) * POOL_GC)
                z = extp_ref[prow - POOL_HP:prow + ROW_BLOCK, lanes]
                s, span = z, 1
                while span < min(w, SUBLANES):
                    s = s + pltpu.roll(s, span, axis=0)
                    span *= 2
                win = s[POOL_HP:]
                for back in range(span, w, span):
                    win = win + s[POOL_HP - back:POOL_HP - back + ROW_BLOCK]
                if pos0 + in_seq >= w - 1:
                    mean = win * (1.0 / w)
                else:
                    pos = pos0 + tile0 + in_seq + lax.broadcasted_iota(jnp.int32, (ROW_BLOCK, POOL_GC), 0)
                    mean = win / jnp.minimum(pos + 1, w).astype(jnp.float32)
                cat_ref[rows, lanes] = (mean - z[POOL_HP:]).astype(jnp.bfloat16)
            for l in range(c * chunk // LANES, (c + 1) * chunk // LANES):
                lanes = slice(l * LANES, (l + 1) * LANES)
                win = extc_ref[crow - CCV_HP:crow + ROW_BLOCK, lanes]
                taps = [ccvw_ref[CCV_K - 1 - d:CCV_K - d, lanes] for d in range(CCV_K)]
                cb_ref[rows, lanes] = _causal_conv(win, CCV_HP, ROW_BLOCK, taps) + ccvb_ref[:, lanes]

    def stage_out(k):
        rows = slice(k * SUB_ROWS, (k + 1) * SUB_ROWS)
        for n in range(k * SUB_ROWS, (k + 1) * SUB_ROWS, NORM_ROWS):
            yb = _silu(_layernorm(cb_ref[n:n + NORM_ROWS, :], lng_ref[...], lnb_ref[...]))
            cat_ref[n:n + NORM_ROWS, POOL_WIDTH:] = yb.astype(jnp.bfloat16)
        for g in range(len(POOL_WINDOWS)):
            lanes = slice(g * POOL_GC, (g + 1) * POOL_GC)
            ya = jnp.dot(cat_ref[rows, lanes], poolw_ref[g], preferred_element_type=jnp.float32)
            cat_ref[rows, lanes] = (ya * pscale_ref[:, lanes]).astype(jnp.bfloat16)
        y = jnp.dot(cat_ref[rows, :], wout_ref[...], preferred_element_type=jnp.float32)
        o_ref[rows, :] = x_ref[rows, :] + y

    h_ref[...] = _rmsnorm_bf16(x_ref[...], g_ref[...])
    nchunk = POOL_WIDTH // chunk
    stage_in(0)
    for c in range(nchunk):
        if c + 1 < nchunk:
            stage_in(c + 1)
        mix_chunk(c)
    _save_hist(extp_ref, stpool_ref, nseg, seg, POOL_HP)
    _save_hist(extc_ref, stccv_ref, nseg, seg, CCV_HP)
    for k in range(tm // SUB_ROWS):
        stage_out(k)


def _odd_kernel(x_ref, g_ref, win_ref, lng_ref, lnb_ref, ws_ref, sb_ref, scw_ref, wout_ref, hsc_ref,
                o_ref, stsc_ref, *rest, nseg, seg, carry, emit_v):
    if emit_v:
        v_ref, ug_ref, exts_ref, vb_ref, cat_ref, wsm_ref = rest
    else:
        v_ref = None
        ug_ref, exts_ref, vb_ref, cat_ref, wsm_ref = rest
    tm = nseg * seg
    w = SGU_WIDTH
    _seed_state(stsc_ref, hsc_ref, carry)
    _load_hist(exts_ref, stsc_ref, nseg, seg, SHORT_HP)

    ch = min(seg, SGU_CHUNK)
    tri = (lax.broadcasted_iota(jnp.int32, (ch, ch), 0) >= lax.broadcasted_iota(jnp.int32, (ch, ch), 1))
    for hd in range(SGU_HEADS):
        wsm_ref[hd] = jnp.where(tri, ws_ref[hd, 0:ch, 0:ch], 0.0).astype(jnp.bfloat16)

    def stage_in(k):
        r0 = k * SUB_ROWS
        h = _rmsnorm_bf16(x_ref[r0:r0 + SUB_ROWS, :], g_ref[...])
        proj = jnp.dot(h, win_ref[...], preferred_element_type=jnp.float32)
        ug_ref[r0:r0 + SUB_ROWS, 0:w] = proj[:, 0:w]
        ug_ref[r0:r0 + SUB_ROWS, w:] = proj[:, 2 * w:3 * w]
        for n in range(0, SUB_ROWS, NORM_ROWS):
            v = _layernorm(proj[n:n + NORM_ROWS, w:2 * w], lng_ref[...], lnb_ref[...])
            if emit_v:
                v_ref[r0 + n:r0 + n + NORM_ROWS, :] = v
            vb_ref[r0 + n:r0 + n + NORM_ROWS, :] = v.astype(jnp.bfloat16)
        _store_rows(exts_ref, proj[:, 3 * w:4 * w] * proj[:, 4 * w:], r0, seg, SHORT_HP)

    def after_in():
        _save_hist(exts_ref, stsc_ref, nseg, seg, SHORT_HP)

    def stage_mix(k):
        for hd in range(SGU_HEADS):
            lanes = slice(hd * SGU_HC, (hd + 1) * SGU_HC)
            bias = jnp.broadcast_to(sb_ref[0:ch, hd:hd + 1], (ch, SGU_HC))
            for row in range(k * SUB_ROWS, (k + 1) * SUB_ROWS, ch):
                rows = slice(row, row + ch)
                mixed = jnp.dot(wsm_ref[hd], vb_ref[rows, lanes], preferred_element_type=jnp.float32) + bias
                cat_ref[rows, lanes] = (ug_ref[rows, lanes] * mixed).astype(jnp.bfloat16)
        for row in range(k * SUB_ROWS, (k + 1) * SUB_ROWS, ROW_BLOCK):
            rows = slice(row, row + ROW_BLOCK)
            erow = _ext_row(row, seg, SHORT_HP)
            for l in range(SCONV_WIDTH // LANES):
                lanes = slice(l * LANES, (l + 1) * LANES)
                win = exts_ref[erow - SHORT_HP:erow + ROW_BLOCK, lanes]
                taps = [scw_ref[SHORT_K - 1 - d:SHORT_K - d, lanes] for d in range(SHORT_K)]
                cz = _causal_conv(win, SHORT_HP, ROW_BLOCK, taps)
                cat_ref[rows, w + l * LANES:w + (l + 1) * LANES] = (
                    ug_ref[rows, w + l * LANES:w + (l + 1) * LANES] * cz).astype(jnp.bfloat16)

    def stage_out(k):
        rows = slice(k * SUB_ROWS, (k + 1) * SUB_ROWS)
        y = jnp.dot(cat_ref[rows, :], wout_ref[...], preferred_element_type=jnp.float32)
        o_ref[rows, :] = x_ref[rows, :] + y

    _pipelined(tm // SUB_ROWS, stage_in, after_in, stage_mix, stage_out)


def _ffn_kernel(x_ref, g_ref, wup_ref, cw_ref, cb_ref, wdown_ref, gfin_ref, hist_ref,
                o_ref, st_ref, xp_ref, h_ref, ext_ref, act_ref, *, carry, final):
    nchunk = D_FF // FF_CHUNK
    hb = (SHORT_K - 1) * SUBLANES
    half_rows = TILE_ROWS // 2
    half_times = TIMES // 2
    _seed_state(st_ref, hist_ref, carry)

    for hf in range(2):
        t0 = hf * half_times
        xs = jnp.stack([x_ref[a * TIMES + t0:a * TIMES + t0 + half_times, :] for a in range(SUBLANES)])
        xp = pltpu.einshape("abc->bac", xs).reshape(half_rows, D_MODEL)
        xp_ref[hf * half_rows:(hf + 1) * half_rows, :] = xp
        h_ref[hf * half_rows:(hf + 1) * half_rows, :] = _rmsnorm_bf16(xp, g_ref[...])

    def up_chunk(j):
        for half in range(2):
            col = half * D_FF + j * FF_CHUNK
            cols = slice(col, col + FF_CHUNK)
            buf = ext_ref.at[j % 2, half]
            if j == 0:
                up = jnp.concatenate(
                    [jnp.dot(h_ref[hf * half_rows:(hf + 1) * half_rows, :], wup_ref[:, cols],
                             preferred_element_type=jnp.float32) for hf in range(2)], axis=0)
            else:
                up = jnp.dot(h_ref[...], wup_ref[:, cols], preferred_element_type=jnp.float32)
            buf[hb:, :] = up
            for d in range(1, SHORT_K):
                last = up[(TIMES - d) * SUBLANES:(TIMES - d + 1) * SUBLANES]
                if carry:
                    first = lax.broadcasted_iota(jnp.int32, last.shape, 0) == 0
                    prev = jnp.where(first, st_ref[SHORT_K - 1 - d, :, cols], pltpu.roll(last, 1, axis=0))
                    st_ref[SHORT_K - 1 - d, :, cols] = last[SUBLANES - 1:]
                else:
                    prev = st_ref[SHORT_K - 1 - d, :, cols]
                    st_ref[SHORT_K - 1 - d, :, cols] = last
                buf[hb - d * SUBLANES:hb - (d - 1) * SUBLANES, :] = prev

    def act_chunk(j):
        for row in range(0, TILE_ROWS, ROW_BLOCK):
            cs = []
            for half in range(2):
                cols = slice(half * D_FF + j * FF_CHUNK, half * D_FF + (j + 1) * FF_CHUNK)
                acc = cb_ref[:, cols]
                for d in range(SHORT_K):
                    start = hb + row - d * SUBLANES
                    acc = acc + (cw_ref[SHORT_K - 1 - d:SHORT_K - d, cols]
                                 * ext_ref[j % 2, half, start:start + ROW_BLOCK, :])
                cs.append(acc)
            a = _silu(cs[0]) * cs[1]
            act_ref[row:row + ROW_BLOCK, j * FF_CHUNK:(j + 1) * FF_CHUNK] = a.astype(jnp.bfloat16)

    up_chunk(0)
    for j in range(1, nchunk):
        up_chunk(j)
        act_chunk(j - 1)
    split = (nchunk - 2) * FF_CHUNK
    ys = [jnp.dot(act_ref[hf * half_rows:(hf + 1) * half_rows, :split], wdown_ref[:split, :],
                  preferred_element_type=jnp.float32) for hf in range(2)]
    act_chunk(nchunk - 1)
    for hf in range(2):
        rows = slice(hf * half_rows, (hf + 1) * half_rows)
        y = ys[hf] + jnp.dot(act_ref[rows, split:], wdown_ref[split:, :], preferred_element_type=jnp.float32)
        y = xp_ref[rows, :] + y
        if final:
            y = y * lax.rsqrt(jnp.mean(y * y, axis=-1, keepdims=True) + EPS) * gfin_ref[...]
        y = pltpu.einshape("bac->abc", y.reshape(half_times, SUBLANES, D_MODEL))
        for a in range(SUBLANES):
            t0 = a * TIMES + hf * half_times
            o_ref[t0:t0 + half_times, :] = y[a]


def _layer_spec(arr, layer):
    zeros = (0,) * (arr.ndim - 1)
    return pl.BlockSpec((None,) + arr.shape[1:], lambda i: (layer,) + zeros, pipeline_mode=pl.Buffered(1))


def _row_spec(width):
    return pl.BlockSpec((TILE_ROWS, width), lambda i: (i, 0))


def _state_spec(nseg, hp, width, carry):
    if carry:
        return pl.BlockSpec((nseg, hp, width), lambda i: (0, 0, 0))
    return pl.BlockSpec((nseg, hp, width), lambda i: (i, 0, 0))


def _params():
    return pltpu.CompilerParams(dimension_semantics=("arbitrary",), vmem_limit_bytes=VMEM_LIMIT_BYTES)


def _geometry(x, seg):
    rows = x.shape[0]
    carry = seg > TILE_ROWS
    tile_seg = TILE_ROWS if carry else seg
    nseg = TILE_ROWS // tile_seg
    assert rows % TILE_ROWS == 0 and tile_seg % ROW_BLOCK == 0 and TILE_ROWS % SUB_ROWS == 0
    assert SUB_ROWS % min(tile_seg, SUB_ROWS) == 0 and ROW_BLOCK % NORM_ROWS == 0
    return rows, carry, tile_seg, nseg


def _even_call(x, seg, pos0, layer, consts, h_pool, h_ccv):
    rows, carry, tseg, nseg = _geometry(x, seg)
    nseq = h_pool.shape[0]
    kern = functools.partial(_even_kernel, nseg=nseg, seg=tseg, carry=carry, pos0=pos0)
    return pl.pallas_call(
        kern,
        grid=(rows // TILE_ROWS,),
        in_specs=[_row_spec(D_MODEL)] + [_layer_spec(c, layer) for c in consts]
        + [_state_spec(nseg, POOL_HP, POOL_WIDTH, carry), _state_spec(nseg, CCV_HP, CCV_WIDTH, carry)],
        out_specs=[_row_spec(D_MODEL), _state_spec(nseg, POOL_HP, POOL_WIDTH, carry),
                   _state_spec(nseg, CCV_HP, CCV_WIDTH, carry)],
        out_shape=[jax.ShapeDtypeStruct((rows, D_MODEL), jnp.float32),
                   jax.ShapeDtypeStruct((nseq, POOL_HP, POOL_WIDTH), jnp.float32),
                   jax.ShapeDtypeStruct((nseq, CCV_HP, CCV_WIDTH), jnp.float32)],
        scratch_shapes=[pltpu.VMEM((TILE_ROWS, D_MODEL), jnp.bfloat16),
                        pltpu.VMEM((nseg * (POOL_HP + tseg), POOL_WIDTH), jnp.float32),
                        pltpu.VMEM((nseg * (CCV_HP + tseg), CCV_WIDTH), jnp.float32),
                        pltpu.VMEM((TILE_ROWS, CCV_WIDTH), jnp.float32),
                        pltpu.VMEM((TILE_ROWS, POOL_WIDTH + CCV_WIDTH), jnp.bfloat16)],
        compiler_params=_params(),
        name="even_mixer",
    )(x, *consts, h_pool, h_ccv)


def _odd_call(x, seg, emit_v, layer, consts, h_sc):
    rows, carry, tseg, nseg = _geometry(x, seg)
    nseq = h_sc.shape[0]
    kern = functools.partial(_odd_kernel, nseg=nseg, seg=tseg, carry=carry, emit_v=emit_v)
    out_specs = [_row_spec(D_MODEL), _state_spec(nseg, SHORT_HP, SCONV_WIDTH, carry)]
    out_shape = [jax.ShapeDtypeStruct((rows, D_MODEL), jnp.float32),
                 jax.ShapeDtypeStruct((nseq, SHORT_HP, SCONV_WIDTH), jnp.float32)]
    if emit_v:
        out_specs.append(_row_spec(SGU_WIDTH))
        out_shape.append(jax.ShapeDtypeStruct((rows, SGU_WIDTH), jnp.float32))
    ch = min(tseg, SGU_CHUNK)
    return pl.pallas_call(
        kern,
        grid=(rows // TILE_ROWS,),
        in_specs=[_row_spec(D_MODEL)] + [_layer_spec(c, layer) for c in consts]
        + [_state_spec(nseg, SHORT_HP, SCONV_WIDTH, carry)],
        out_specs=out_specs,
        out_shape=out_shape,
        scratch_shapes=[pltpu.VMEM((TILE_ROWS, SGU_WIDTH + SCONV_WIDTH), jnp.float32),
                        pltpu.VMEM((nseg * (SHORT_HP + tseg), SCONV_WIDTH), jnp.float32),
                        pltpu.VMEM((TILE_ROWS, SGU_WIDTH), jnp.bfloat16),
                        pltpu.VMEM((TILE_ROWS, SGU_WIDTH + SCONV_WIDTH), jnp.bfloat16),
                        pltpu.VMEM((SGU_HEADS, ch, ch), jnp.bfloat16)],
        compiler_params=_params(),
        name="odd_mixer",
    )(x, *consts, h_sc)


def _ffn_call(x, seg, final, layer, consts, g_final, hist):
    rows, carry, tseg, nseg = _geometry(x, seg)
    assert carry or tseg == TIMES
    nseq = hist.shape[2]
    streams = 1 if carry else SUBLANES
    kern = functools.partial(_ffn_kernel, carry=carry, final=final)
    hist_block = (SHORT_K - 1, streams, 2 * D_FF)
    tile_or_first = (lambda i: 0) if carry else (lambda i: i)
    return pl.pallas_call(
        kern,
        grid=(rows // TILE_ROWS,),
        in_specs=[_row_spec(D_MODEL)] + [_layer_spec(c, layer) for c in consts]
        + [_layer_spec(g_final, 0),
           pl.BlockSpec((None,) + hist_block, lambda i: (layer, 0, tile_or_first(i), 0))],
        out_specs=[_row_spec(D_MODEL), pl.BlockSpec(hist_block, lambda i: (0, tile_or_first(i), 0))],
        out_shape=[jax.ShapeDtypeStruct((rows, D_MODEL), jnp.float32),
                   jax.ShapeDtypeStruct((SHORT_K - 1, nseq, 2 * D_FF), jnp.float32)],
        scratch_shapes=[pltpu.VMEM((TILE_ROWS, D_MODEL), jnp.float32),
                        pltpu.VMEM((TILE_ROWS, D_MODEL), jnp.bfloat16),
                        pltpu.VMEM((2, 2, (SHORT_K - 1) * SUBLANES + TILE_ROWS, FF_CHUNK), jnp.float32),
                        pltpu.VMEM((TILE_ROWS, D_FF), jnp.bfloat16)],
        compiler_params=_params(),
        name="conv_ffn",
    )(x, *consts, g_final, hist)


def _pad_hist(hist, hp):
    return jnp.pad(hist, ((0, 0), (hp - hist.shape[1], 0), (0, 0)))


def _trunk(x, seg, pos0, emit_v, st_pool, st_ccv, st_sconv, st_ffn, even, odd, ffn, g_final):
    depth = ffn[1].shape[0]
    new_pool, new_ccv, new_sconv, new_v, new_ffn = [], [], [], [], []
    for i in range(depth):
        j = i // 2
        if i % 2 == 0:
            x, npool, nccv = _even_call(x, seg, pos0, j, even,
                                        _pad_hist(st_pool[j], POOL_HP), _pad_hist(st_ccv[j], CCV_HP))
            new_pool.append(npool[:, POOL_HP - st_pool.shape[2]:])
            new_ccv.append(nccv[:, CCV_HP - st_ccv.shape[2]:])
        else:
            res = _odd_call(x, seg, emit_v, j, odd, _pad_hist(st_sconv[j], SHORT_HP))
            x = res[0]
            new_sconv.append(res[1][:, SHORT_HP - st_sconv.shape[2]:])
            if emit_v:
                new_v.append(res[2])
        x, nffn = _ffn_call(x, seg, i == depth - 1, i, ffn, g_final, st_ffn)
        new_ffn.append(nffn)
    return x, new_pool, new_ccv, new_sconv, new_v, new_ffn


def kernel(x_prompt, x_sample, state_pool, state_ccv, state_sconv, state_ffn_conv, norm_mix_g, norm_ffn_g, norm_final_g, w_in_even, pool_w, pool_scale, ccv_w, ccv_b, ccv_ln_g, ccv_ln_b, w_out_even, w_in_odd, sgu_ln_g, sgu_ln_b, sgu_ws, sgu_b, sconv_w, w_out_odd, ffn_w_up, ffn_conv_w, ffn_conv_b, ffn_w_down):
    bf16 = jnp.bfloat16
    row = lambda a: a[:, None, :]
    even = (row(norm_mix_g[0::2]), w_in_even.astype(bf16), pool_w.astype(bf16), row(pool_scale), ccv_w, row(ccv_b),
            row(ccv_ln_g), row(ccv_ln_b), w_out_even.astype(bf16))
    odd = (row(norm_mix_g[1::2]), w_in_odd.astype(bf16), row(sgu_ln_g), row(sgu_ln_b), sgu_ws, jnp.swapaxes(sgu_b, 1, 2),
           sconv_w, w_out_odd.astype(bf16))
    ffn = (row(norm_ffn_g), ffn_w_up.astype(bf16), ffn_conv_w, row(ffn_conv_b), ffn_w_down.astype(bf16))
    g_final = norm_final_g[None, None, :]
    n_pair, depth = w_in_even.shape[0], ffn_w_up.shape[0]
    b, seq, _ = x_prompt.shape
    db, dseq, _ = x_sample.shape
    assert b == 1
    f32 = x_prompt.dtype

    zeros = lambda n, k, c: jnp.zeros((n, b, k, c), f32)
    yp, pool_p, ccv_p, sconv_p, _, ffn_p = _trunk(
        x_prompt.reshape(b * seq, D_MODEL), seq, 0, False,
        zeros(n_pair, state_pool.shape[2], POOL_WIDTH), zeros(n_pair, state_ccv.shape[2], CCV_WIDTH),
        zeros(n_pair, state_sconv.shape[2], SCONV_WIDTH), jnp.zeros((depth, SHORT_K - 1, b, 2 * D_FF), f32),
        even, odd, ffn, g_final)
    ys, pool_s, ccv_s, sconv_s, v_s, ffn_s = _trunk(
        x_sample.reshape(db * dseq, D_MODEL), dseq, PAST_LEN, True,
        state_pool, state_ccv, state_sconv, jnp.swapaxes(state_ffn_conv, 1, 2), even, odd, ffn, g_final)
    v_s = [v.reshape(db, dseq, SGU_WIDTH) for v in v_s]
    return (yp.reshape(b, seq, D_MODEL), ys.reshape(db, dseq, D_MODEL),
            jnp.stack(pool_p), jnp.stack(pool_s), jnp.stack(ccv_p), jnp.stack(ccv_s),
            jnp.stack(sconv_p), jnp.stack(sconv_s), jnp.stack(v_s),
            jnp.swapaxes(jnp.stack(ffn_p), 1, 2), jnp.swapaxes(jnp.stack(ffn_s), 1, 2))
```

```python
import functools

import jax
import jax.numpy as jnp
from jax import lax
from jax.experimental import pallas as pl
from jax.experimental.pallas import tpu as pltpu

D_MODEL = 1024
PAST_LEN = 4096
POOL_GC = 128
POOL_WINDOWS = (2, 4, 8, 16)
POOL_WIDTH = 512
CCV_WIDTH = 512
CCV_K = 31
SGU_HEADS = 4
SGU_HC = 128
SGU_WIDTH = 512
SGU_CHUNK = 128
SCONV_WIDTH = 512
SHORT_K = 3
D_FF = 2816
EPS = 1e-6

LANES = 128
SUBLANES = 8
TILE_ROWS = 512
TIMES = TILE_ROWS // SUBLANES
SUB_ROWS = 256
ROW_BLOCK = 64
NORM_ROWS = 32
FF_CHUNK = 256
POOL_HP = 16
CCV_HP = 32
SHORT_HP = 8
VMEM_LIMIT_BYTES = 56 * 1024 * 1024


def _rmsnorm_bf16(x, g):
    y = x * lax.rsqrt(jnp.mean(x * x, axis=-1, keepdims=True) + EPS)
    return (y * g).astype(jnp.bfloat16)


def _layernorm(x, g, b):
    mu = jnp.mean(x, axis=-1, keepdims=True)
    xc = x - mu
    var = jnp.mean(xc * xc, axis=-1, keepdims=True)
    return xc * lax.rsqrt(var + EPS) * g + b


def _silu(x):
    return x * jax.nn.sigmoid(x)


def _seed_state(st_ref, hist_ref, carry):
    if carry:
        @pl.when(pl.program_id(0) == 0)
        def _():
            st_ref[...] = hist_ref[...]
    else:
        st_ref[...] = hist_ref[...]


def _ext_row(row, seg, hp):
    s, r = divmod(row, seg)
    return s * (hp + seg) + hp + r


def _store_rows(ext_ref, val, row0, seg, hp):
    n = val.shape[0]
    run = min(seg, n)
    for off in range(0, n, run):
        ext_ref[pl.ds(_ext_row(row0 + off, seg, hp), run), :] = val[off:off + run]


def _causal_conv(win, hp, nrows, taps):
    acc = None
    for r in range(min(len(taps), SUBLANES)):
        rolled = win if r == 0 else pltpu.roll(win, r, axis=0)
        for d in range(r, len(taps), SUBLANES):
            start = hp - (d - r)
            term = taps[d] * rolled[start:start + nrows]
            acc = term if acc is None else acc + term
    return acc


def _load_hist(ext_ref, st_ref, nseg, seg, hp):
    for s in range(nseg):
        ext_ref[pl.ds(s * (hp + seg), hp), :] = st_ref[s]


def _save_hist(ext_ref, st_ref, nseg, seg, hp):
    for s in range(nseg):
        st_ref[s] = ext_ref[pl.ds(s * (hp + seg) + seg, hp), :]


def _pipelined(nsub, stage_in, after_in, stage_mix, stage_out):
    stage_in(0)
    for k in range(nsub):
        if k + 1 < nsub:
            stage_in(k + 1)
        if k == max(nsub - 2, 0):
            after_in()
        stage_mix(k)
        stage_out(k)


def _even_kernel(x_ref, g_ref, win_ref, poolw_ref, pscale_ref, ccvw_ref, ccvb_ref, lng_ref,
                 lnb_ref, wout_ref, hpool_ref, hccv_ref,
                 o_ref, stpool_ref, stccv_ref,
                 h_ref, extp_ref, extc_ref, cb_ref, cat_ref,
                 *, nseg, seg, carry, pos0):
    tm = nseg * seg
    _seed_state(stpool_ref, hpool_ref, carry)
    _seed_state(stccv_ref, hccv_ref, carry)
    _load_hist(extp_ref, stpool_ref, nseg, seg, POOL_HP)
    _load_hist(extc_ref, stccv_ref, nseg, seg, CCV_HP)
    tile0 = pl.program_id(0) * tm if carry else 0
    chunk = 2 * LANES

    def store_rows(ext_ref, val, hp, cols):
        for s0 in range(nseg):
            ext_ref[pl.ds(s0 * (hp + seg) + hp, seg), cols] = val[s0 * seg:(s0 + 1) * seg]

    def stage_in(c):
        cols = slice(c * chunk, (c + 1) * chunk)
        proj = [jnp.dot(h_ref[...], win_ref[:, p * POOL_WIDTH + c * chunk:p * POOL_WIDTH + (c + 1) * chunk],
                        preferred_element_type=jnp.float32) for p in range(3)]
        store_rows(extp_ref, proj[0], POOL_HP, cols)
        store_rows(extc_ref, proj[1] * jax.nn.sigmoid(proj[2]), CCV_HP, cols)

    def mix_chunk(c):
        for row in range(0, tm, ROW_BLOCK):
            rows = slice(row, row + ROW_BLOCK)
            in_seq = row % seg
            prow = _ext_row(row, seg, POOL_HP)
            crow = _ext_row(row, seg, CCV_HP)
            for g in range(c * chunk // POOL_GC, (c + 1) * chunk // POOL_GC):
                w = POOL_WINDOWS[g]
                lanes = slice(g * POOL_GC, (g + 1) * POOL_GC)
                z = extp_ref[prow - POOL_HP:prow + ROW_BLOCK, lanes]
                s, span = z, 1
                while span < min(w, SUBLANES):
                    s = s + pltpu.roll(s, span, axis=0)
                    span *= 2
                win = s[POOL_HP:]
                for back in range(span, w, span):
                    win = win + s[POOL_HP - back:POOL_HP - back + ROW_BLOCK]
                if pos0 + in_seq >= w - 1:
                    mean = win * (1.0 / w)
                else:
                    pos = pos0 + tile0 + in_seq + lax.broadcasted_iota(jnp.int32, (ROW_BLOCK, POOL_GC), 0)
                    mean = win / jnp.minimum(pos + 1, w).astype(jnp.float32)
                cat_ref[rows, lanes] = (mean - z[POOL_HP:]).astype(jnp.bfloat16)
            for l in range(c * chunk // LANES, (c + 1) * chunk // LANES):
                lanes = slice(l * LANES, (l + 1) * LANES)
                win = extc_ref[crow - CCV_HP:crow + ROW_BLOCK, lanes]
                taps = [ccvw_ref[CCV_K - 1 - d:CCV_K - d, lanes] for d in range(CCV_K)]
                cb_ref[rows, lanes] = _causal_conv(win, CCV_HP, ROW_BLOCK, taps) + ccvb_ref[:, lanes]

    def stage_out(k):
        rows = slice(k * SUB_ROWS, (k + 1) * SUB_ROWS)
        for n in range(k * SUB_ROWS, (k + 1) * SUB_ROWS, NORM_ROWS):
            yb = _silu(_layernorm(cb_ref[n:n + NORM_ROWS, :], lng_ref[...], lnb_ref[...]))
            cat_ref[n:n + NORM_ROWS, POOL_WIDTH:] = yb.astype(jnp.bfloat16)
        for g in range(len(POOL_WINDOWS)):
            lanes = slice(g * POOL_GC, (g + 1) * POOL_GC)
            ya = jnp.dot(cat_ref[rows, lanes], poolw_ref[g], preferred_element_type=jnp.float32)
            cat_ref[rows, lanes] = (ya * pscale_ref[:, lanes]).astype(jnp.bfloat16)
        y = jnp.dot(cat_ref[rows, :], wout_ref[...], preferred_element_type=jnp.float32)
        o_ref[rows, :] = x_ref[rows, :] + y

    h_ref[...] = _rmsnorm_bf16(x_ref[...], g_ref[...])
    nchunk = POOL_WIDTH // chunk
    stage_in(0)
    for c in range(nchunk):
        if c + 1 < nchunk:
            stage_in(c + 1)
        mix_chunk(c)
    _save_hist(extp_ref, stpool_ref, nseg, seg, POOL_HP)
    _save_hist(extc_ref, stccv_ref, nseg, seg, CCV_HP)
    for k in range(tm // SUB_ROWS):
        stage_out(k)


def _odd_kernel(x_ref, g_ref, win_ref, lng_ref, lnb_ref, ws_ref, sb_ref, scw_ref, wout_ref, hsc_ref,
                o_ref, stsc_ref, *rest, nseg, seg, carry, emit_v):
    if emit_v:
        v_ref, ug_ref, exts_ref, vb_ref, cat_ref, wsm_ref = rest
    else:
        v_ref = None
        ug_ref, exts_ref, vb_ref, cat_ref, wsm_ref = rest
    tm = nseg * seg
    w = SGU_WIDTH
    _seed_state(stsc_ref, hsc_ref, carry)
    _load_hist(exts_ref, stsc_ref, nseg, seg, SHORT_HP)

    ch = min(seg, SGU_CHUNK)
    tri = (lax.broadcasted_iota(jnp.int32, (ch, ch), 0) >= lax.broadcasted_iota(jnp.int32, (ch, ch), 1))
    for hd in range(SGU_HEADS):
        wsm_ref[hd] = jnp.where(tri, ws_ref[hd, 0:ch, 0:ch], 0.0).astype(jnp.bfloat16)

    def stage_in(k):
        r0 = k * SUB_ROWS
        h = _rmsnorm_bf16(x_ref[r0:r0 + SUB_ROWS, :], g_ref[...])
        part = lambda p: jnp.dot(h, win_ref[:, p * w:(p + 1) * w], preferred_element_type=jnp.float32)
        vraw = part(1)
        for n in range(0, SUB_ROWS, NORM_ROWS):
            v = _layernorm(vraw[n:n + NORM_ROWS], lng_ref[...], lnb_ref[...])
            if emit_v:
                v_ref[r0 + n:r0 + n + NORM_ROWS, :] = v
            vb_ref[r0 + n:r0 + n + NORM_ROWS, :] = v.astype(jnp.bfloat16)
        _store_rows(exts_ref, part(3) * part(4), r0, seg, SHORT_HP)
        ug_ref[r0:r0 + SUB_ROWS, 0:w] = part(0)
        ug_ref[r0:r0 + SUB_ROWS, w:] = part(2)

    def after_in():
        _save_hist(exts_ref, stsc_ref, nseg, seg, SHORT_HP)

    def stage_mix(k):
        for hd in range(SGU_HEADS):
            lanes = slice(hd * SGU_HC, (hd + 1) * SGU_HC)
            bias = jnp.broadcast_to(sb_ref[0:ch, hd:hd + 1], (ch, SGU_HC))
            for row in range(k * SUB_ROWS, (k + 1) * SUB_ROWS, ch):
                rows = slice(row, row + ch)
                mixed = jnp.dot(wsm_ref[hd], vb_ref[rows, lanes], preferred_element_type=jnp.float32) + bias
                cat_ref[rows, lanes] = (ug_ref[rows, lanes] * mixed).astype(jnp.bfloat16)
        for row in range(k * SUB_ROWS, (k + 1) * SUB_ROWS, ROW_BLOCK):
            rows = slice(row, row + ROW_BLOCK)
            erow = _ext_row(row, seg, SHORT_HP)
            for l in range(SCONV_WIDTH // LANES):
                lanes = slice(l * LANES, (l + 1) * LANES)
                win = exts_ref[erow - SHORT_HP:erow + ROW_BLOCK, lanes]
                taps = [scw_ref[SHORT_K - 1 - d:SHORT_K - d, lanes] for d in range(SHORT_K)]
                cz = _causal_conv(win, SHORT_HP, ROW_BLOCK, taps)
                cat_ref[rows, w + l * LANES:w + (l + 1) * LANES] = (
                    ug_ref[rows, w + l * LANES:w + (l + 1) * LANES] * cz).astype(jnp.bfloat16)

    def stage_out(k):
        rows = slice(k * SUB_ROWS, (k + 1) * SUB_ROWS)
        y = jnp.dot(cat_ref[rows, :], wout_ref[...], preferred_element_type=jnp.float32)
        o_ref[rows, :] = x_ref[rows, :] + y

    _pipelined(tm // SUB_ROWS, stage_in, after_in, stage_mix, stage_out)


def _ffn_kernel(x_ref, g_ref, wup_ref, cw_ref, cb_ref, wdown_ref, gfin_ref, hist_ref,
                o_ref, st_ref, xp_ref, h_ref, ext_ref, act_ref, *, carry, final):
    nchunk = D_FF // FF_CHUNK
    hb = (SHORT_K - 1) * SUBLANES
    half_rows = TILE_ROWS // 2
    half_times = TIMES // 2
    _seed_state(st_ref, hist_ref, carry)

    for hf in range(2):
        t0 = hf * half_times
        xs = jnp.stack([x_ref[a * TIMES + t0:a * TIMES + t0 + half_times, :] for a in range(SUBLANES)])
        xp = pltpu.einshape("abc->bac", xs).reshape(half_rows, D_MODEL)
        xp_ref[hf * half_rows:(hf + 1) * half_rows, :] = xp
        h_ref[hf * half_rows:(hf + 1) * half_rows, :] = _rmsnorm_bf16(xp, g_ref[...])

    def up_chunk(j):
        for half in range(2):
            col = half * D_FF + j * FF_CHUNK
            cols = slice(col, col + FF_CHUNK)
            buf = ext_ref.at[j % 2, half]
            if j == 0:
                up = jnp.concatenate(
                    [jnp.dot(h_ref[hf * half_rows:(hf + 1) * half_rows, :], wup_ref[:, cols],
                             preferred_element_type=jnp.float32) for hf in range(2)], axis=0)
            else:
                up = jnp.dot(h_ref[...], wup_ref[:, cols], preferred_element_type=jnp.float32)
            buf[hb:, :] = up
            for d in range(1, SHORT_K):
                last = up[(TIMES - d) * SUBLANES:(TIMES - d + 1) * SUBLANES]
                if carry:
                    first = lax.broadcasted_iota(jnp.int32, last.shape, 0) == 0
                    prev = jnp.where(first, st_ref[SHORT_K - 1 - d, :, cols], pltpu.roll(last, 1, axis=0))
                    st_ref[SHORT_K - 1 - d, :, cols] = last[SUBLANES - 1:]
                else:
                    prev = st_ref[SHORT_K - 1 - d, :, cols]
                    st_ref[SHORT_K - 1 - d, :, cols] = last
                buf[hb - d * SUBLANES:hb - (d - 1) * SUBLANES, :] = prev

    def act_chunk(j):
        for row in range(0, TILE_ROWS, ROW_BLOCK):
            cs = []
            for half in range(2):
                cols = slice(half * D_FF + j * FF_CHUNK, half * D_FF + (j + 1) * FF_CHUNK)
                acc = cb_ref[:, cols]
                for d in range(SHORT_K):
                    start = hb + row - d * SUBLANES
                    acc = acc + (cw_ref[SHORT_K - 1 - d:SHORT_K - d, cols]
                                 * ext_ref[j % 2, half, start:start + ROW_BLOCK, :])
                cs.append(acc)
            a = _silu(cs[0]) * cs[1]
            act_ref[row:row + ROW_BLOCK, j * FF_CHUNK:(j + 1) * FF_CHUNK] = a.astype(jnp.bfloat16)

    up_chunk(0)
    for j in range(1, nchunk):
        up_chunk(j)
        act_chunk(j - 1)
    split = (nchunk - 2) * FF_CHUNK
    ys = [jnp.dot(act_ref[hf * half_rows:(hf + 1) * half_rows, :split], wdown_ref[:split, :],
                  preferred_element_type=jnp.float32) for hf in range(2)]
    act_chunk(nchunk - 1)
    for hf in range(2):
        rows = slice(hf * half_rows, (hf + 1) * half_rows)
        y = ys[hf] + jnp.dot(act_ref[rows, split:], wdown_ref[split:, :], preferred_element_type=jnp.float32)
        y = xp_ref[rows, :] + y
        if final:
            y = y * lax.rsqrt(jnp.mean(y * y, axis=-1, keepdims=True) + EPS) * gfin_ref[...]
        y = pltpu.einshape("bac->abc", y.reshape(half_times, SUBLANES, D_MODEL))
        for a in range(SUBLANES):
            t0 = a * TIMES + hf * half_times
            o_ref[t0:t0 + half_times, :] = y[a]


def _layer_spec(arr, layer):
    zeros = (0,) * (arr.ndim - 1)
    return pl.BlockSpec((None,) + arr.shape[1:], lambda i: (layer,) + zeros, pipeline_mode=pl.Buffered(1))


def _row_spec(width):
    return pl.BlockSpec((TILE_ROWS, width), lambda i: (i, 0))


def _state_spec(nseg, hp, width, carry):
    if carry:
        return pl.BlockSpec((nseg, hp, width), lambda i: (0, 0, 0))
    return pl.BlockSpec((nseg, hp, width), lambda i: (i, 0, 0))


def _params():
    return pltpu.CompilerParams(dimension_semantics=("arbitrary",), vmem_limit_bytes=VMEM_LIMIT_BYTES)


def _geometry(x, seg):
    rows = x.shape[0]
    carry = seg > TILE_ROWS
    tile_seg = TILE_ROWS if carry else seg
    nseg = TILE_ROWS // tile_seg
    assert rows % TILE_ROWS == 0 and tile_seg % ROW_BLOCK == 0 and TILE_ROWS % SUB_ROWS == 0
    assert SUB_ROWS % min(tile_seg, SUB_ROWS) == 0 and ROW_BLOCK % NORM_ROWS == 0
    return rows, carry, tile_seg, nseg


def _even_call(x, seg, pos0, layer, consts, h_pool, h_ccv):
    rows, carry, tseg, nseg = _geometry(x, seg)
    nseq = h_pool.shape[0]
    kern = functools.partial(_even_kernel, nseg=nseg, seg=tseg, carry=carry, pos0=pos0)
    return pl.pallas_call(
        kern,
        grid=(rows // TILE_ROWS,),
        in_specs=[_row_spec(D_MODEL)] + [_layer_spec(c, layer) for c in consts]
        + [_state_spec(nseg, POOL_HP, POOL_WIDTH, carry), _state_spec(nseg, CCV_HP, CCV_WIDTH, carry)],
        out_specs=[_row_spec(D_MODEL), _state_spec(nseg, POOL_HP, POOL_WIDTH, carry),
                   _state_spec(nseg, CCV_HP, CCV_WIDTH, carry)],
        out_shape=[jax.ShapeDtypeStruct((rows, D_MODEL), jnp.float32),
                   jax.ShapeDtypeStruct((nseq, POOL_HP, POOL_WIDTH), jnp.float32),
                   jax.ShapeDtypeStruct((nseq, CCV_HP, CCV_WIDTH), jnp.float32)],
        scratch_shapes=[pltpu.VMEM((TILE_ROWS, D_MODEL), jnp.bfloat16),
                        pltpu.VMEM((nseg * (POOL_HP + tseg), POOL_WIDTH), jnp.float32),
                        pltpu.VMEM((nseg * (CCV_HP + tseg), CCV_WIDTH), jnp.float32),
                        pltpu.VMEM((TILE_ROWS, CCV_WIDTH), jnp.float32),
                        pltpu.VMEM((TILE_ROWS, POOL_WIDTH + CCV_WIDTH), jnp.bfloat16)],
        compiler_params=_params(),
        name="even_mixer",
    )(x, *consts, h_pool, h_ccv)


def _odd_call(x, seg, emit_v, layer, consts, h_sc):
    rows, carry, tseg, nseg = _geometry(x, seg)
    nseq = h_sc.shape[0]
    kern = functools.partial(_odd_kernel, nseg=nseg, seg=tseg, carry=carry, emit_v=emit_v)
    out_specs = [_row_spec(D_MODEL), _state_spec(nseg, SHORT_HP, SCONV_WIDTH, carry)]
    out_shape = [jax.ShapeDtypeStruct((rows, D_MODEL), jnp.float32),
                 jax.ShapeDtypeStruct((nseq, SHORT_HP, SCONV_WIDTH), jnp.float32)]
    if emit_v:
        out_specs.append(_row_spec(SGU_WIDTH))
        out_shape.append(jax.ShapeDtypeStruct((rows, SGU_WIDTH), jnp.float32))
    ch = min(tseg, SGU_CHUNK)
    return pl.pallas_call(
        kern,
        grid=(rows // TILE_ROWS,),
        in_specs=[_row_spec(D_MODEL)] + [_layer_spec(c, layer) for c in consts]
        + [_state_spec(nseg, SHORT_HP, SCONV_WIDTH, carry)],
        out_specs=out_specs,
        out_shape=out_shape,
        scratch_shapes=[pltpu.VMEM((TILE_ROWS, SGU_WIDTH + SCONV_WIDTH), jnp.float32),
                        pltpu.VMEM((nseg * (SHORT_HP + tseg), SCONV_WIDTH), jnp.float32),
                        pltpu.VMEM((TILE_ROWS, SGU_WIDTH), jnp.bfloat16),
                        pltpu.VMEM((TILE_ROWS, SGU_WIDTH + SCONV_WIDTH), jnp.bfloat16),
                        pltpu.VMEM((SGU_HEADS, ch, ch), jnp.bfloat16)],
        compiler_params=_params(),
        name="odd_mixer",
    )(x, *consts, h_sc)


def _ffn_call(x, seg, final, layer, consts, g_final, hist):
    rows, carry, tseg, nseg = _geometry(x, seg)
    assert carry or tseg == TIMES
    nseq = hist.shape[2]
    streams = 1 if carry else SUBLANES
    kern = functools.partial(_ffn_kernel, carry=carry, final=final)
    hist_block = (SHORT_K - 1, streams, 2 * D_FF)
    tile_or_first = (lambda i: 0) if carry else (lambda i: i)
    return pl.pallas_call(
        kern,
        grid=(rows // TILE_ROWS,),
        in_specs=[_row_spec(D_MODEL)] + [_layer_spec(c, layer) for c in consts]
        + [_layer_spec(g_final, 0),
           pl.BlockSpec((None,) + hist_block, lambda i: (layer, 0, tile_or_first(i), 0))],
        out_specs=[_row_spec(D_MODEL), pl.BlockSpec(hist_block, lambda i: (0, tile_or_first(i), 0))],
        out_shape=[jax.ShapeDtypeStruct((rows, D_MODEL), jnp.float32),
                   jax.ShapeDtypeStruct((SHORT_K - 1, nseq, 2 * D_FF), jnp.float32)],
        scratch_shapes=[pltpu.VMEM((TILE_ROWS, D_MODEL), jnp.float32),
                        pltpu.VMEM((TILE_ROWS, D_MODEL), jnp.bfloat16),
                        pltpu.VMEM((2, 2, (SHORT_K - 1) * SUBLANES + TILE_ROWS, FF_CHUNK), jnp.float32),
                        pltpu.VMEM((TILE_ROWS, D_FF), jnp.bfloat16)],
        compiler_params=_params(),
        name="conv_ffn",
    )(x, *consts, g_final, hist)


def _pad_hist(hist, hp):
    return jnp.pad(hist, ((0, 0), (hp - hist.shape[1], 0), (0, 0)))


def _trunk(x, seg, pos0, emit_v, st_pool, st_ccv, st_sconv, st_ffn, even, odd, ffn, g_final):
    depth = ffn[1].shape[0]
    new_pool, new_ccv, new_sconv, new_v, new_ffn = [], [], [], [], []
    for i in range(depth):
        j = i // 2
        if i % 2 == 0:
            x, npool, nccv = _even_call(x, seg, pos0, j, even,
                                        _pad_hist(st_pool[j], POOL_HP), _pad_hist(st_ccv[j], CCV_HP))
            new_pool.append(npool[:, POOL_HP - st_pool.shape[2]:])
            new_ccv.append(nccv[:, CCV_HP - st_ccv.shape[2]:])
        else:
            res = _odd_call(x, seg, emit_v, j, odd, _pad_hist(st_sconv[j], SHORT_HP))
            x = res[0]
            new_sconv.append(res[1][:, SHORT_HP - st_sconv.shape[2]:])
            if emit_v:
                new_v.append(res[2])
        x, nffn = _ffn_call(x, seg, i == depth - 1, i, ffn, g_final, st_ffn)
        new_ffn.append(nffn)
    return x, new_pool, new_ccv, new_sconv, new_v, new_ffn


def kernel(x_prompt, x_sample, state_pool, state_ccv, state_sconv, state_ffn_conv, norm_mix_g, norm_ffn_g, norm_final_g, w_in_even, pool_w, pool_scale, ccv_w, ccv_b, ccv_ln_g, ccv_ln_b, w_out_even, w_in_odd, sgu_ln_g, sgu_ln_b, sgu_ws, sgu_b, sconv_w, w_out_odd, ffn_w_up, ffn_conv_w, ffn_conv_b, ffn_w_down):
    bf16 = jnp.bfloat16
    row = lambda a: a[:, None, :]
    even = (row(norm_mix_g[0::2]), w_in_even.astype(bf16), pool_w.astype(bf16), row(pool_scale), ccv_w, row(ccv_b),
            row(ccv_ln_g), row(ccv_ln_b), w_out_even.astype(bf16))
    odd = (row(norm_mix_g[1::2]), w_in_odd.astype(bf16), row(sgu_ln_g), row(sgu_ln_b), sgu_ws, jnp.swapaxes(sgu_b, 1, 2),
           sconv_w, w_out_odd.astype(bf16))
    ffn = (row(norm_ffn_g), ffn_w_up.astype(bf16), ffn_conv_w, row(ffn_conv_b), ffn_w_down.astype(bf16))
    g_final = norm_final_g[None, None, :]
    n_pair, depth = w_in_even.shape[0], ffn_w_up.shape[0]
    b, seq, _ = x_prompt.shape
    db, dseq, _ = x_sample.shape
    assert b == 1
    f32 = x_prompt.dtype

    zeros = lambda n, k, c: jnp.zeros((n, b, k, c), f32)
    yp, pool_p, ccv_p, sconv_p, _, ffn_p = _trunk(
        x_prompt.reshape(b * seq, D_MODEL), seq, 0, False,
        zeros(n_pair, state_pool.shape[2], POOL_WIDTH), zeros(n_pair, state_ccv.shape[2], CCV_WIDTH),
        zeros(n_pair, state_sconv.shape[2], SCONV_WIDTH), jnp.zeros((depth, SHORT_K - 1, b, 2 * D_FF), f32),
        even, odd, ffn, g_final)
    ys, pool_s, ccv_s, sconv_s, v_s, ffn_s = _trunk(
        x_sample.reshape(db * dseq, D_MODEL), dseq, PAST_LEN, True,
        state_pool, state_ccv, state_sconv, jnp.swapaxes(state_ffn_conv, 1, 2), even, odd, ffn, g_final)
    v_s = [v.reshape(db, dseq, SGU_WIDTH) for v in v_s]
    return (yp.reshape(b, seq, D_MODEL), ys.reshape(db, dseq, D_MODEL),
            jnp.stack(pool_p), jnp.stack(pool_s), jnp.stack(ccv_p), jnp.stack(ccv_s),
            jnp.stack(sconv_p), jnp.stack(sconv_s), jnp.stack(v_s),
            jnp.swapaxes(jnp.stack(ffn_p), 1, 2), jnp.swapaxes(jnp.stack(ffn_s), 1, 2))
```

```python
import functools

import jax
import jax.numpy as jnp
from jax import lax
from jax.experimental import pallas as pl
from jax.experimental.pallas import tpu as pltpu

D_MODEL = 1024
PAST_LEN = 4096
POOL_GC = 128
POOL_WINDOWS = (2, 4, 8, 16)
POOL_WIDTH = 512
CCV_WIDTH = 512
CCV_K = 31
SGU_HEADS = 4
SGU_HC = 128
SGU_WIDTH = 512
SGU_CHUNK = 128
SCONV_WIDTH = 512
SHORT_K = 3
D_FF = 2816
EPS = 1e-6

LANES = 128
SUBLANES = 8
TILE_ROWS = 512
TIMES = TILE_ROWS // SUBLANES
SUB_ROWS = 256
ROW_BLOCK = 64
NORM_ROWS = 32
FF_CHUNK = 256
POOL_HP = 16
CCV_HP = 32
SHORT_HP = 8
VMEM_LIMIT_BYTES = 56 * 1024 * 1024


def _rmsnorm_bf16(x, g):
    y = x * lax.rsqrt(jnp.mean(x * x, axis=-1, keepdims=True) + EPS)
    return (y * g).astype(jnp.bfloat16)


def _layernorm(x, g, b):
    mu = jnp.mean(x, axis=-1, keepdims=True)
    xc = x - mu
    var = jnp.mean(xc * xc, axis=-1, keepdims=True)
    return xc * lax.rsqrt(var + EPS) * g + b


def _silu(x):
    return x * jax.nn.sigmoid(x)


def _seed_state(st_ref, hist_ref, carry):
    if carry:
        @pl.when(pl.program_id(0) == 0)
        def _():
            st_ref[...] = hist_ref[...]
    else:
        st_ref[...] = hist_ref[...]


def _ext_row(row, seg, hp):
    s, r = divmod(row, seg)
    return s * (hp + seg) + hp + r


def _store_rows(ext_ref, val, row0, seg, hp):
    n = val.shape[0]
    run = min(seg, n)
    for off in range(0, n, run):
        ext_ref[pl.ds(_ext_row(row0 + off, seg, hp), run), :] = val[off:off + run]


def _causal_conv(win, hp, nrows, taps):
    acc = None
    for r in range(min(len(taps), SUBLANES)):
        rolled = win if r == 0 else pltpu.roll(win, r, axis=0)
        for d in range(r, len(taps), SUBLANES):
            start = hp - (d - r)
            term = taps[d] * rolled[start:start + nrows]
            acc = term if acc is None else acc + term
    return acc


def _load_hist(ext_ref, st_ref, nseg, seg, hp):
    for s in range(nseg):
        ext_ref[pl.ds(s * (hp + seg), hp), :] = st_ref[s]


def _save_hist(ext_ref, st_ref, nseg, seg, hp):
    for s in range(nseg):
        st_ref[s] = ext_ref[pl.ds(s * (hp + seg) + seg, hp), :]


def _pipelined(nsub, stage_in, after_in, stage_mix, stage_out):
    stage_in(0)
    for k in range(nsub):
        if k + 1 < nsub:
            stage_in(k + 1)
        if k == max(nsub - 2, 0):
            after_in()
        stage_mix(k)
        stage_out(k)


def _even_kernel(x_ref, g_ref, win_ref, poolw_ref, pscale_ref, ccvw_ref, ccvb_ref, lng_ref,
                 lnb_ref, wout_ref, hpool_ref, hccv_ref,
                 o_ref, stpool_ref, stccv_ref,
                 h_ref, extp_ref, extc_ref, cb_ref, cat_ref,
                 *, nseg, seg, carry, pos0):
    tm = nseg * seg
    _seed_state(stpool_ref, hpool_ref, carry)
    _seed_state(stccv_ref, hccv_ref, carry)
    _load_hist(extp_ref, stpool_ref, nseg, seg, POOL_HP)
    _load_hist(extc_ref, stccv_ref, nseg, seg, CCV_HP)
    tile0 = pl.program_id(0) * tm if carry else 0
    chunk = 2 * LANES

    def store_rows(ext_ref, val, hp, cols):
        for s0 in range(nseg):
            ext_ref[pl.ds(s0 * (hp + seg) + hp, seg), cols] = val[s0 * seg:(s0 + 1) * seg]

    def stage_in(c):
        cols = slice(c * chunk, (c + 1) * chunk)
        proj = [jnp.dot(h_ref[...], win_ref[:, p * POOL_WIDTH + c * chunk:p * POOL_WIDTH + (c + 1) * chunk],
                        preferred_element_type=jnp.float32) for p in range(3)]
        store_rows(extp_ref, proj[0], POOL_HP, cols)
        store_rows(extc_ref, proj[1] * jax.nn.sigmoid(proj[2]), CCV_HP, cols)

    def mix_chunk(c):
        for row in range(0, tm, ROW_BLOCK):
            rows = slice(row, row + ROW_BLOCK)
            in_seq = row % seg
            prow = _ext_row(row, seg, POOL_HP)
            crow = _ext_row(row, seg, CCV_HP)
            for g in range(c * chunk // POOL_GC, (c + 1) * chunk // POOL_GC):
                w = POOL_WINDOWS[g]
                lanes = slice(g * POOL_GC, (g + 1) * POOL_GC)
                z = extp_ref[prow - POOL_HP:prow + ROW_BLOCK, lanes]
                s, span = z, 1
                while span < min(w, SUBLANES):
                    s = s + pltpu.roll(s, span, axis=0)
                    span *= 2
                win = s[POOL_HP:]
                for back in range(span, w, span):
                    win = win + s[POOL_HP - back:POOL_HP - back + ROW_BLOCK]
                if pos0 + in_seq >= w - 1:
                    mean = win * (1.0 / w)
                else:
                    pos = pos0 + tile0 + in_seq + lax.broadcasted_iota(jnp.int32, (ROW_BLOCK, POOL_GC), 0)
                    mean = win / jnp.minimum(pos + 1, w).astype(jnp.float32)
                cat_ref[rows, lanes] = (mean - z[POOL_HP:]).astype(jnp.bfloat16)
            for l in range(c * chunk // LANES, (c + 1) * chunk // LANES):
                lanes = slice(l * LANES, (l + 1) * LANES)
                win = extc_ref[crow - CCV_HP:crow + ROW_BLOCK, lanes]
                taps = [ccvw_ref[CCV_K - 1 - d:CCV_K - d, lanes] for d in range(CCV_K)]
                cb_ref[rows, lanes] = _causal_conv(win, CCV_HP, ROW_BLOCK, taps) + ccvb_ref[:, lanes]

    def stage_out(k):
        rows = slice(k * SUB_ROWS, (k + 1) * SUB_ROWS)
        for n in range(k * SUB_ROWS, (k + 1) * SUB_ROWS, NORM_ROWS):
            yb = _silu(_layernorm(cb_ref[n:n + NORM_ROWS, :], lng_ref[...], lnb_ref[...]))
            cat_ref[n:n + NORM_ROWS, POOL_WIDTH:] = yb.astype(jnp.bfloat16)
        for g in range(len(POOL_WINDOWS)):
            lanes = slice(g * POOL_GC, (g + 1) * POOL_GC)
            ya = jnp.dot(cat_ref[rows, lanes], poolw_ref[g], preferred_element_type=jnp.float32)
            cat_ref[rows, lanes] = (ya * pscale_ref[:, lanes]).astype(jnp.bfloat16)
        y = jnp.dot(cat_ref[rows, :], wout_ref[...], preferred_element_type=jnp.float32)
        o_ref[rows, :] = x_ref[rows, :] + y

    h_ref[...] = _rmsnorm_bf16(x_ref[...], g_ref[...])
    nchunk = POOL_WIDTH // chunk
    stage_in(0)
    for c in range(nchunk):
        if c + 1 < nchunk:
            stage_in(c + 1)
        mix_chunk(c)
    _save_hist(extp_ref, stpool_ref, nseg, seg, POOL_HP)
    _save_hist(extc_ref, stccv_ref, nseg, seg, CCV_HP)
    for k in range(tm // SUB_ROWS):
        stage_out(k)


def _odd_kernel(x_ref, g_ref, win_ref, lng_ref, lnb_ref, ws_ref, sb_ref, scw_ref, wout_ref, hsc_ref,
                o_ref, stsc_ref, *rest, nseg, seg, carry, emit_v):
    if emit_v:
        v_ref, ug_ref, exts_ref, vb_ref, cat_ref, wsm_ref = rest
    else:
        v_ref = None
        ug_ref, exts_ref, vb_ref, cat_ref, wsm_ref = rest
    tm = nseg * seg
    w = SGU_WIDTH
    _seed_state(stsc_ref, hsc_ref, carry)
    _load_hist(exts_ref, stsc_ref, nseg, seg, SHORT_HP)

    ch = min(seg, SGU_CHUNK)
    tri = (lax.broadcasted_iota(jnp.int32, (ch, ch), 0) >= lax.broadcasted_iota(jnp.int32, (ch, ch), 1))
    for hd in range(SGU_HEADS):
        wsm_ref[hd] = jnp.where(tri, ws_ref[hd, 0:ch, 0:ch], 0.0).astype(jnp.bfloat16)

    def stage_in(k):
        r0 = k * SUB_ROWS
        h = _rmsnorm_bf16(x_ref[r0:r0 + SUB_ROWS, :], g_ref[...])
        part = lambda p: jnp.dot(h, win_ref[:, p * w:(p + 1) * w], preferred_element_type=jnp.float32)
        vraw = part(1)
        for n in range(0, SUB_ROWS, NORM_ROWS):
            v = _layernorm(vraw[n:n + NORM_ROWS], lng_ref[...], lnb_ref[...])
            if emit_v:
                v_ref[r0 + n:r0 + n + NORM_ROWS, :] = v
            vb_ref[r0 + n:r0 + n + NORM_ROWS, :] = v.astype(jnp.bfloat16)
        _store_rows(exts_ref, part(3) * part(4), r0, seg, SHORT_HP)
        ug_ref[r0:r0 + SUB_ROWS, 0:w] = part(0)
        ug_ref[r0:r0 + SUB_ROWS, w:] = part(2)

    def after_in():
        _save_hist(exts_ref, stsc_ref, nseg, seg, SHORT_HP)

    def stage_mix(k):
        for hd in range(SGU_HEADS):
            lanes = slice(hd * SGU_HC, (hd + 1) * SGU_HC)
            bias = jnp.broadcast_to(sb_ref[0:ch, hd:hd + 1], (ch, SGU_HC))
            for row in range(k * SUB_ROWS, (k + 1) * SUB_ROWS, ch):
                rows = slice(row, row + ch)
                mixed = jnp.dot(wsm_ref[hd], vb_ref[rows, lanes], preferred_element_type=jnp.float32) + bias
                cat_ref[rows, lanes] = (ug_ref[rows, lanes] * mixed).astype(jnp.bfloat16)
        for row in range(k * SUB_ROWS, (k + 1) * SUB_ROWS, ROW_BLOCK):
            rows = slice(row, row + ROW_BLOCK)
            erow = _ext_row(row, seg, SHORT_HP)
            for l in range(SCONV_WIDTH // LANES):
                lanes = slice(l * LANES, (l + 1) * LANES)
                win = exts_ref[erow - SHORT_HP:erow + ROW_BLOCK, lanes]
                taps = [scw_ref[SHORT_K - 1 - d:SHORT_K - d, lanes] for d in range(SHORT_K)]
                cz = _causal_conv(win, SHORT_HP, ROW_BLOCK, taps)
                cat_ref[rows, w + l * LANES:w + (l + 1) * LANES] = (
                    ug_ref[rows, w + l * LANES:w + (l + 1) * LANES] * cz).astype(jnp.bfloat16)

    def stage_out(k):
        rows = slice(k * SUB_ROWS, (k + 1) * SUB_ROWS)
        y = jnp.dot(cat_ref[rows, :], wout_ref[...], preferred_element_type=jnp.float32)
        o_ref[rows, :] = x_ref[rows, :] + y

    _pipelined(tm // SUB_ROWS, stage_in, after_in, stage_mix, stage_out)


def _ffn_kernel(x_ref, g_ref, wup_ref, cw_ref, cb_ref, wdown_ref, gfin_ref, hist_ref,
                o_ref, st_ref, xp_ref, h_ref, ext_ref, act_ref, *, carry, final):
    nchunk = D_FF // FF_CHUNK
    hb = (SHORT_K - 1) * SUBLANES
    half_rows = TILE_ROWS // 2
    half_times = TIMES // 2
    _seed_state(st_ref, hist_ref, carry)

    for hf in range(2):
        t0 = hf * half_times
        xs = jnp.stack([x_ref[a * TIMES + t0:a * TIMES + t0 + half_times, :] for a in range(SUBLANES)])
        xp = pltpu.einshape("abc->bac", xs).reshape(half_rows, D_MODEL)
        xp_ref[hf * half_rows:(hf + 1) * half_rows, :] = xp
        h_ref[hf * half_rows:(hf + 1) * half_rows, :] = _rmsnorm_bf16(xp, g_ref[...])

    def up_chunk(j):
        for half in range(2):
            col = half * D_FF + j * FF_CHUNK
            cols = slice(col, col + FF_CHUNK)
            buf = ext_ref.at[j % 2, half]
            if j == 0:
                up = jnp.concatenate(
                    [jnp.dot(h_ref[hf * half_rows:(hf + 1) * half_rows, :], wup_ref[:, cols],
                             preferred_element_type=jnp.float32) for hf in range(2)], axis=0)
            else:
                up = jnp.dot(h_ref[...], wup_ref[:, cols], preferred_element_type=jnp.float32)
            buf[hb:, :] = up
            for d in range(1, SHORT_K):
                last = up[(TIMES - d) * SUBLANES:(TIMES - d + 1) * SUBLANES]
                if carry:
                    first = lax.broadcasted_iota(jnp.int32, last.shape, 0) == 0
                    prev = jnp.where(first, st_ref[SHORT_K - 1 - d, :, cols], pltpu.roll(last, 1, axis=0))
                    st_ref[SHORT_K - 1 - d, :, cols] = last[SUBLANES - 1:]
                else:
                    prev = st_ref[SHORT_K - 1 - d, :, cols]
                    st_ref[SHORT_K - 1 - d, :, cols] = last
                buf[hb - d * SUBLANES:hb - (d - 1) * SUBLANES, :] = prev

    def act_chunk(j):
        for row in range(0, TILE_ROWS, ROW_BLOCK):
            cs = []
            for half in range(2):
                cols = slice(half * D_FF + j * FF_CHUNK, half * D_FF + (j + 1) * FF_CHUNK)
                acc = cb_ref[:, cols]
                for d in range(SHORT_K):
                    start = hb + row - d * SUBLANES
                    acc = acc + (cw_ref[SHORT_K - 1 - d:SHORT_K - d, cols]
                                 * ext_ref[j % 2, half, start:start + ROW_BLOCK, :])
                cs.append(acc)
            a = _silu(cs[0]) * cs[1]
            act_ref[row:row + ROW_BLOCK, j * FF_CHUNK:(j + 1) * FF_CHUNK] = a.astype(jnp.bfloat16)

    up_chunk(0)
    for j in range(1, nchunk):
        up_chunk(j)
        act_chunk(j - 1)
    split = (nchunk - 2) * FF_CHUNK
    ys = [jnp.dot(act_ref[hf * half_rows:(hf + 1) * half_rows, :split], wdown_ref[:split, :],
                  preferred_element_type=jnp.float32) for hf in range(2)]
    act_chunk(nchunk - 1)
    for hf in range(2):
        rows = slice(hf * half_rows, (hf + 1) * half_rows)
        y = ys[hf] + jnp.dot(act_ref[rows, split:], wdown_ref[split:, :], preferred_element_type=jnp.float32)
        y = xp_ref[rows, :] + y
        if final:
            y = y * lax.rsqrt(jnp.mean(y * y, axis=-1, keepdims=True) + EPS) * gfin_ref[...]
        y = pltpu.einshape("bac->abc", y.reshape(half_times, SUBLANES, D_MODEL))
        for a in range(SUBLANES):
            t0 = a * TIMES + hf * half_times
            o_ref[t0:t0 + half_times, :] = y[a]


def _layer_spec(arr, layer):
    zeros = (0,) * (arr.ndim - 1)
    return pl.BlockSpec((None,) + arr.shape[1:], lambda i: (layer,) + zeros, pipeline_mode=pl.Buffered(1))


def _row_spec(width):
    return pl.BlockSpec((TILE_ROWS, width), lambda i: (i, 0))


def _state_spec(nseg, hp, width, carry):
    if carry:
        return pl.BlockSpec((nseg, hp, width), lambda i: (0, 0, 0))
    return pl.BlockSpec((nseg, hp, width), lambda i: (i, 0, 0))


def _params(operands):
    return pltpu.CompilerParams(dimension_semantics=("arbitrary",), vmem_limit_bytes=VMEM_LIMIT_BYTES,
                                allow_input_fusion=[o.dtype == jnp.bfloat16 for o in operands])


def _geometry(x, seg):
    rows = x.shape[0]
    carry = seg > TILE_ROWS
    tile_seg = TILE_ROWS if carry else seg
    nseg = TILE_ROWS // tile_seg
    assert rows % TILE_ROWS == 0 and tile_seg % ROW_BLOCK == 0 and TILE_ROWS % SUB_ROWS == 0
    assert SUB_ROWS % min(tile_seg, SUB_ROWS) == 0 and ROW_BLOCK % NORM_ROWS == 0
    return rows, carry, tile_seg, nseg


def _even_call(x, seg, pos0, layer, consts, h_pool, h_ccv):
    rows, carry, tseg, nseg = _geometry(x, seg)
    nseq = h_pool.shape[0]
    kern = functools.partial(_even_kernel, nseg=nseg, seg=tseg, carry=carry, pos0=pos0)
    return pl.pallas_call(
        kern,
        grid=(rows // TILE_ROWS,),
        in_specs=[_row_spec(D_MODEL)] + [_layer_spec(c, layer) for c in consts]
        + [_state_spec(nseg, POOL_HP, POOL_WIDTH, carry), _state_spec(nseg, CCV_HP, CCV_WIDTH, carry)],
        out_specs=[_row_spec(D_MODEL), _state_spec(nseg, POOL_HP, POOL_WIDTH, carry),
                   _state_spec(nseg, CCV_HP, CCV_WIDTH, carry)],
        out_shape=[jax.ShapeDtypeStruct((rows, D_MODEL), jnp.float32),
                   jax.ShapeDtypeStruct((nseq, POOL_HP, POOL_WIDTH), jnp.float32),
                   jax.ShapeDtypeStruct((nseq, CCV_HP, CCV_WIDTH), jnp.float32)],
        scratch_shapes=[pltpu.VMEM((TILE_ROWS, D_MODEL), jnp.bfloat16),
                        pltpu.VMEM((nseg * (POOL_HP + tseg), POOL_WIDTH), jnp.float32),
                        pltpu.VMEM((nseg * (CCV_HP + tseg), CCV_WIDTH), jnp.float32),
                        pltpu.VMEM((TILE_ROWS, CCV_WIDTH), jnp.float32),
                        pltpu.VMEM((TILE_ROWS, POOL_WIDTH + CCV_WIDTH), jnp.bfloat16)],
        compiler_params=_params((x, *consts, h_pool, h_ccv)),
        name="even_mixer",
    )(x, *consts, h_pool, h_ccv)


def _odd_call(x, seg, emit_v, layer, consts, h_sc):
    rows, carry, tseg, nseg = _geometry(x, seg)
    nseq = h_sc.shape[0]
    kern = functools.partial(_odd_kernel, nseg=nseg, seg=tseg, carry=carry, emit_v=emit_v)
    out_specs = [_row_spec(D_MODEL), _state_spec(nseg, SHORT_HP, SCONV_WIDTH, carry)]
    out_shape = [jax.ShapeDtypeStruct((rows, D_MODEL), jnp.float32),
                 jax.ShapeDtypeStruct((nseq, SHORT_HP, SCONV_WIDTH), jnp.float32)]
    if emit_v:
        out_specs.append(_row_spec(SGU_WIDTH))
        out_shape.append(jax.ShapeDtypeStruct((rows, SGU_WIDTH), jnp.float32))
    ch = min(tseg, SGU_CHUNK)
    return pl.pallas_call(
        kern,
        grid=(rows // TILE_ROWS,),
        in_specs=[_row_spec(D_MODEL)] + [_layer_spec(c, layer) for c in consts]
        + [_state_spec(nseg, SHORT_HP, SCONV_WIDTH, carry)],
        out_specs=out_specs,
        out_shape=out_shape,
        scratch_shapes=[pltpu.VMEM((TILE_ROWS, SGU_WIDTH + SCONV_WIDTH), jnp.float32),
                        pltpu.VMEM((nseg * (SHORT_HP + tseg), SCONV_WIDTH), jnp.float32),
                        pltpu.VMEM((TILE_ROWS, SGU_WIDTH), jnp.bfloat16),
                        pltpu.VMEM((TILE_ROWS, SGU_WIDTH + SCONV_WIDTH), jnp.bfloat16),
                        pltpu.VMEM((SGU_HEADS, ch, ch), jnp.bfloat16)],
        compiler_params=_params((x, *consts, h_sc)),
        name="odd_mixer",
    )(x, *consts, h_sc)


def _ffn_call(x, seg, final, layer, consts, g_final, hist):
    rows, carry, tseg, nseg = _geometry(x, seg)
    assert carry or tseg == TIMES
    nseq = hist.shape[2]
    streams = 1 if carry else SUBLANES
    kern = functools.partial(_ffn_kernel, carry=carry, final=final)
    hist_block = (SHORT_K - 1, streams, 2 * D_FF)
    tile_or_first = (lambda i: 0) if carry else (lambda i: i)
    return pl.pallas_call(
        kern,
        grid=(rows // TILE_ROWS,),
        in_specs=[_row_spec(D_MODEL)] + [_layer_spec(c, layer) for c in consts]
        + [_layer_spec(g_final, 0),
           pl.BlockSpec((None,) + hist_block, lambda i: (layer, 0, tile_or_first(i), 0))],
        out_specs=[_row_spec(D_MODEL), pl.BlockSpec(hist_block, lambda i: (0, tile_or_first(i), 0))],
        out_shape=[jax.ShapeDtypeStruct((rows, D_MODEL), jnp.float32),
                   jax.ShapeDtypeStruct((SHORT_K - 1, nseq, 2 * D_FF), jnp.float32)],
        scratch_shapes=[pltpu.VMEM((TILE_ROWS, D_MODEL), jnp.float32),
                        pltpu.VMEM((TILE_ROWS, D_MODEL), jnp.bfloat16),
                        pltpu.VMEM((2, 2, (SHORT_K - 1) * SUBLANES + TILE_ROWS, FF_CHUNK), jnp.float32),
                        pltpu.VMEM((TILE_ROWS, D_FF), jnp.bfloat16)],
        compiler_params=_params((x, *consts, g_final, hist)),
        name="conv_ffn",
    )(x, *consts, g_final, hist)


def _pad_hist(hist, hp):
    return jnp.pad(hist, ((0, 0), (hp - hist.shape[1], 0), (0, 0)))


def _trunk(x, seg, pos0, emit_v, st_pool, st_ccv, st_sconv, st_ffn, even, odd, ffn, g_final):
    depth = ffn[1].shape[0]
    new_pool, new_ccv, new_sconv, new_v, new_ffn = [], [], [], [], []
    for i in range(depth):
        j = i // 2
        if i % 2 == 0:
            x, npool, nccv = _even_call(x, seg, pos0, j, even,
                                        _pad_hist(st_pool[j], POOL_HP), _pad_hist(st_ccv[j], CCV_HP))
            new_pool.append(npool[:, POOL_HP - st_pool.shape[2]:])
            new_ccv.append(nccv[:, CCV_HP - st_ccv.shape[2]:])
        else:
            res = _odd_call(x, seg, emit_v, j, odd, _pad_hist(st_sconv[j], SHORT_HP))
            x = res[0]
            new_sconv.append(res[1][:, SHORT_HP - st_sconv.shape[2]:])
            if emit_v:
                new_v.append(res[2])
        x, nffn = _ffn_call(x, seg, i == depth - 1, i, ffn, g_final, st_ffn)
        new_ffn.append(nffn)
    return x, new_pool, new_ccv, new_sconv, new_v, new_ffn


def kernel(x_prompt, x_sample, state_pool, state_ccv, state_sconv, state_ffn_conv, norm_mix_g, norm_ffn_g, norm_final_g, w_in_even, pool_w, pool_scale, ccv_w, ccv_b, ccv_ln_g, ccv_ln_b, w_out_even, w_in_odd, sgu_ln_g, sgu_ln_b, sgu_ws, sgu_b, sconv_w, w_out_odd, ffn_w_up, ffn_conv_w, ffn_conv_b, ffn_w_down):
    bf16 = jnp.bfloat16
    row = lambda a: a[:, None, :]
    even = (row(norm_mix_g[0::2]), w_in_even.astype(bf16), pool_w.astype(bf16), row(pool_scale), ccv_w, row(ccv_b),
            row(ccv_ln_g), row(ccv_ln_b), w_out_even.astype(bf16))
    odd = (row(norm_mix_g[1::2]), w_in_odd.astype(bf16), row(sgu_ln_g), row(sgu_ln_b), sgu_ws, jnp.swapaxes(sgu_b, 1, 2),
           sconv_w, w_out_odd.astype(bf16))
    ffn = (row(norm_ffn_g), ffn_w_up.astype(bf16), ffn_conv_w, row(ffn_conv_b), ffn_w_down.astype(bf16))
    g_final = norm_final_g[None, None, :]
    n_pair, depth = w_in_even.shape[0], ffn_w_up.shape[0]
    b, seq, _ = x_prompt.shape
    db, dseq, _ = x_sample.shape
    assert b == 1
    f32 = x_prompt.dtype

    zeros = lambda n, k, c: jnp.zeros((n, b, k, c), f32)
    yp, pool_p, ccv_p, sconv_p, _, ffn_p = _trunk(
        x_prompt.reshape(b * seq, D_MODEL), seq, 0, False,
        zeros(n_pair, state_pool.shape[2], POOL_WIDTH), zeros(n_pair, state_ccv.shape[2], CCV_WIDTH),
        zeros(n_pair, state_sconv.shape[2], SCONV_WIDTH), jnp.zeros((depth, SHORT_K - 1, b, 2 * D_FF), f32),
        even, odd, ffn, g_final)
    ys, pool_s, ccv_s, sconv_s, v_s, ffn_s = _trunk(
        x_sample.reshape(db * dseq, D_MODEL), dseq, PAST_LEN, True,
        state_pool, state_ccv, state_sconv, jnp.swapaxes(state_ffn_conv, 1, 2), even, odd, ffn, g_final)
    v_s = [v.reshape(db, dseq, SGU_WIDTH) for v in v_s]
    return (yp.reshape(b, seq, D_MODEL), ys.reshape(db, dseq, D_MODEL),
            jnp.stack(pool_p), jnp.stack(pool_s), jnp.stack(ccv_p), jnp.stack(ccv_s),
            jnp.stack(sconv_p), jnp.stack(sconv_s), jnp.stack(v_s),
            jnp.swapaxes(jnp.stack(ffn_p), 1, 2), jnp.swapaxes(jnp.stack(ffn_s), 1, 2))
```
